```python
import jax, jax.numpy as jnp
from jax import lax
import numpy as np

D_MODEL = 1024
BATCH = 2
SEQ = 8192
DEPTH = 4
DEC_BATCH = 128
DEC_SEQ = 1
PAST_LEN = 8192
PAGE_SIZE = 128

N_A_LAYERS = DEPTH // 2
N_B_LAYERS = DEPTH - N_A_LAYERS
MIX_WIDTH = D_MODEL
XATTN_HEADS = 4
XATTN_WIDTH = MIX_WIDTH // 4
XATTN_HEAD_DIM = XATTN_WIDTH // XATTN_HEADS
N_MEM = 256
HGRN_WIDTH = MIX_WIDTH - XATTN_WIDTH
HGRN_HEAD_DIM = 128
HGRN_HEADS = HGRN_WIDTH // HGRN_HEAD_DIM
HGRN_CHUNK = 64
SWA_HEAD_DIM = 64
SWA_WIDTH = MIX_WIDTH - XATTN_WIDTH
SWA_HEADS = SWA_WIDTH // SWA_HEAD_DIM
SWA_KV_HEADS = 4
SWA_GROUP = SWA_HEADS // SWA_KV_HEADS
SWA_KV_WIDTH = SWA_KV_HEADS * SWA_HEAD_DIM
WINDOW = 128
ROPE_THETA = 10000.0
D_FF = 4 * D_MODEL
EPS = 1e-6

kernel_name = 'yoco_hgrn2_swa_sink_memxattn_step'


def rms_norm(x, gain):
    xf = x.astype(jnp.float32)
    y = xf * lax.rsqrt(jnp.mean(xf * xf, axis=-1, keepdims=True) + EPS)
    return (y * gain.astype(jnp.float32)).astype(x.dtype)


def rope(x, pos):
    half = x.shape[-1] // 2
    inv = ROPE_THETA ** (-jnp.arange(half, dtype=jnp.float32) / half)
    ang = pos.astype(jnp.float32)[:, None] * inv[None, :]
    cos = jnp.cos(ang)[None, :, None, :]
    sin = jnp.sin(ang)[None, :, None, :]
    xf = x.astype(jnp.float32)
    x1, x2 = xf[..., :half], xf[..., half:]
    return jnp.concatenate([x1 * cos - x2 * sin, x2 * cos + x1 * sin], axis=-1).astype(x.dtype)


def hgrn2_scan(q, k, v, log_f, s0):
    B, L, H, _ = q.shape
    C = min(HGRN_CHUNK, L)
    n = -(-L // C)
    pad = n * C - L

    def prep(a):
        a = jnp.pad(a.astype(jnp.float32), ((0, 0), (0, pad), (0, 0), (0, 0)))
        return a.reshape(B, n, C, H, a.shape[-1]).transpose(1, 0, 3, 2, 4)

    qc, kc, vc, gc = prep(q), prep(k), prep(v), prep(log_f)
    causal = jnp.tril(jnp.ones((C, C), dtype=bool))[:, :, None]

    def step(S, inp):
        qi, ki, vi, gi = inp
        b = jnp.cumsum(gi, axis=2)
        o_inter = jnp.einsum('bhtk,bhkv->bhtv', qi * jnp.exp(b), S)
        diff = b[:, :, :, None, :] - b[:, :, None, :, :]
        decay = jnp.exp(jnp.where(causal, diff, -jnp.inf))
        scores = jnp.einsum('bhtk,bhsk,bhtsk->bhts', qi, ki, decay)
        o = o_inter + jnp.einsum('bhts,bhsv->bhtv', scores, vi)
        b_last = b[:, :, -1:, :]
        S_new = jnp.exp(b_last[:, :, 0, :])[..., None] * S + jnp.einsum(
            'bhsk,bhsv->bhkv', ki * jnp.exp(b_last - b), vi)
        return S_new, o

    S_fin, o = lax.scan(step, s0.astype(jnp.float32), (qc, kc, vc, gc))
    o = o.transpose(1, 0, 3, 2, 4).reshape(B, n * C, H, -1)[:, :L]
    return o, S_fin.astype(s0.dtype)


def hgrn2_mix(proj, lb, g_gain, s0):
    B, L, _ = proj.shape
    q, f_logit, i_in, g = jnp.split(proj, 4, axis=-1)
    shp = (B, L, HGRN_HEADS, HGRN_HEAD_DIM)
    q = jax.nn.silu(q).reshape(shp)
    z = f_logit.astype(jnp.float32).reshape(shp)
    lbh = lb.astype(jnp.float32).reshape(HGRN_HEADS, HGRN_HEAD_DIM)
    log_f = jnp.log(lbh + (1.0 - lbh) * jax.nn.sigmoid(z))
    k = (1.0 - lbh) * jax.nn.sigmoid(-z)
    o, S = hgrn2_scan(q, k, i_in.reshape(shp), log_f, s0)
    o = rms_norm(o, g_gain.reshape(HGRN_HEADS, HGRN_HEAD_DIM)).reshape(B, L, HGRN_WIDTH)
    return (o * jax.nn.silu(g.astype(jnp.float32))).astype(proj.dtype), S


def swa_attend(q, k_ext, v_ext, sinks, base):
    B, L, H, d = q.shape
    QB = min(WINDOW, L)
    n = -(-L // QB)
    pad = n * QB - L
    padw = ((0, 0), (0, pad), (0, 0), (0, 0))
    q = jnp.pad(q, padw)
    k_ext = jnp.pad(k_ext, padw)
    v_ext = jnp.pad(v_ext, padw)
    kidx = jnp.arange(n)[:, None] * QB + jnp.arange(WINDOW + QB)[None, :]
    kb = k_ext[:, kidx]
    vb = v_ext[:, kidx]
    qb = q.reshape(B, n, QB, SWA_KV_HEADS, SWA_GROUP, d)
    qpos = base + jnp.arange(n * QB).reshape(n, QB)
    kpos = base - WINDOW + kidx
    rel = qpos[:, :, None] - kpos[:, None, :]
    mask = (rel >= 0) & (rel < WINDOW) & (kpos[:, None, :] >= 0)
    s = jnp.einsum('bnqhgd,bnkhd->bnhgqk', qb, kb,
                   preferred_element_type=jnp.float32) * (d ** -0.5)
    s = jnp.where(mask[None, :, None, None], s, -jnp.inf)
    sink = sinks.astype(jnp.float32).reshape(SWA_KV_HEADS, SWA_GROUP)[None, None, :, :, None, None]
    m = jnp.maximum(jnp.max(s, axis=-1, keepdims=True), sink)
    p = jnp.exp(s - m)
    den = jnp.sum(p, axis=-1, keepdims=True) + jnp.exp(sink - m)
    o = jnp.einsum('bnhgqk,bnkhd->bnqhgd', (p / den).astype(vb.dtype), vb)
    return o.reshape(B, n * QB, H, d)[:, :L]


def mem_attend(q, mk, mv):
    s = jnp.einsum('blhd,bmhd->bhlm', q, mk, preferred_element_type=jnp.float32) * (q.shape[-1] ** -0.5)
    p = jax.nn.softmax(s, axis=-1).astype(mv.dtype)
    return jnp.einsum('bhlm,bmhd->blhd', p, mv)


def lower_bounds(lb_logits):
    c = jnp.cumsum(jax.nn.softmax(lb_logits.astype(jnp.float32), axis=0), axis=0)
    return c - c[0:1]


def trunk(x, base, mem_k, mem_v, hgrn_s0, swa_k_past, swa_v_past,
          w_in_a, lb_logits, hgrn_norm, w_in_b, sinks, kv_norm, w_kv,
          w_out, norm_pre_mix, norm_post_mix, norm_pre_mlp, norm_post_mlp, w_up, w_down):
    B, L, _ = x.shape
    pos = base + jnp.arange(L)
    lbs = lower_bounds(lb_logits)
    h = x
    new_hgrn = []
    k_ext = swa_k_past
    v_ext = swa_v_past
    for l in range(DEPTH):
        if l == N_A_LAYERS:
            kv = rms_norm(h, kv_norm) @ w_kv
            k_new = rope(kv[..., :SWA_KV_WIDTH].reshape(B, L, SWA_KV_HEADS, SWA_HEAD_DIM), pos)
            v_new = kv[..., SWA_KV_WIDTH:].reshape(B, L, SWA_KV_HEADS, SWA_HEAD_DIM)
            k_ext = jnp.concatenate([swa_k_past, k_new.astype(swa_k_past.dtype)], axis=1)
            v_ext = jnp.concatenate([swa_v_past, v_new.astype(swa_v_past.dtype)], axis=1)
        hn = rms_norm(h, norm_pre_mix[l])
        if l < N_A_LAYERS:
            proj = hn @ w_in_a[l]
            tok_out, s_fin = hgrn2_mix(proj[..., :4 * HGRN_WIDTH], lbs[l], hgrn_norm[l], hgrn_s0[l])
            new_hgrn.append(s_fin)
            cq = proj[..., 4 * HGRN_WIDTH:]
        else:
            j = l - N_A_LAYERS
            proj = hn @ w_in_b[j]
            q = rope(proj[..., :SWA_WIDTH].reshape(B, L, SWA_HEADS, SWA_HEAD_DIM), pos)
            tok_out = swa_attend(q, k_ext.astype(q.dtype), v_ext.astype(q.dtype), sinks[j], base).reshape(B, L, SWA_WIDTH)
            cq = proj[..., SWA_WIDTH:]
        x_out = mem_attend(cq.reshape(B, L, XATTN_HEADS, XATTN_HEAD_DIM),
                           mem_k[l].astype(cq.dtype), mem_v[l].astype(cq.dtype)).reshape(B, L, XATTN_WIDTH)
        mix = jnp.concatenate([tok_out, x_out], axis=-1) @ w_out[l]
        h = h + rms_norm(mix, norm_post_mix[l])
        u = jnp.square(jax.nn.relu(rms_norm(h, norm_pre_mlp[l]) @ w_up[l]))
        h = h + rms_norm(u @ w_down[l], norm_post_mlp[l])
    return h, jnp.stack(new_hgrn), k_ext[:, -WINDOW:], v_ext[:, -WINDOW:]


def setup_inputs(seed: int = 0) -> dict:
    key = jax.random.key(seed)
    ks = iter(jax.random.split(key, 32))

    def nrm(shape, scale):
        return jax.random.normal(next(ks), shape, jnp.float32) * scale

    def gain(shape):
        return 1.0 + 0.05 * jax.random.normal(next(ks), shape, jnp.float32)

    mem_shape_c = (DEPTH, DEC_BATCH, N_MEM, XATTN_HEADS, XATTN_HEAD_DIM)
    swa_shape_c = (DEC_BATCH, WINDOW, SWA_KV_HEADS, SWA_HEAD_DIM)
    return {
        'x_prompt': nrm((BATCH, SEQ, D_MODEL), 1.0),
        'x_sample': nrm((DEC_BATCH, DEC_SEQ, D_MODEL), 1.0),
        'cache_mem_k': nrm(mem_shape_c, 1.0),
        'cache_mem_v': nrm(mem_shape_c, 1.0),
        'state_hgrn': nrm((N_A_LAYERS, DEC_BATCH, HGRN_HEADS, HGRN_HEAD_DIM, HGRN_HEAD_DIM), 0.5),
        'state_swa_k': nrm(swa_shape_c, 1.0),
        'state_swa_v': nrm(swa_shape_c, 1.0),
        'mem_prompt': nrm((BATCH, N_MEM, D_MODEL), 1.0),
        'mem_norm': gain((DEPTH, D_MODEL)),
        'w_mem_kv': nrm((DEPTH, D_MODEL, 2 * XATTN_WIDTH), D_MODEL ** -0.5),
        'w_in_a': nrm((N_A_LAYERS, D_MODEL, 4 * HGRN_WIDTH + XATTN_WIDTH), D_MODEL ** -0.5),
        'lb_logits': nrm((N_A_LAYERS, HGRN_WIDTH), 1.0),
        'hgrn_norm': gain((N_A_LAYERS, HGRN_WIDTH)),
        'w_in_b': nrm((N_B_LAYERS, D_MODEL, SWA_WIDTH + XATTN_WIDTH), D_MODEL ** -0.5),
        'sinks': nrm((N_B_LAYERS, SWA_HEADS), 0.5),
        'kv_norm': gain((D_MODEL,)),
        'w_kv': nrm((D_MODEL, 2 * SWA_KV_WIDTH), D_MODEL ** -0.5),
        'w_out': nrm((DEPTH, MIX_WIDTH, D_MODEL), MIX_WIDTH ** -0.5),
        'norm_pre_mix': gain((DEPTH, D_MODEL)),
        'norm_post_mix': gain((DEPTH, D_MODEL)),
        'norm_pre_mlp': gain((DEPTH, D_MODEL)),
        'norm_post_mlp': gain((DEPTH, D_MODEL)),
        'w_up': nrm((DEPTH, D_MODEL, D_FF), D_MODEL ** -0.5),
        'w_down': nrm((DEPTH, D_FF, D_MODEL), D_FF ** -0.5),
    }


def reference(x_prompt, x_sample, cache_mem_k, cache_mem_v, state_hgrn, state_swa_k, state_swa_v,
              mem_prompt, mem_norm, w_mem_kv, w_in_a, lb_logits, hgrn_norm, w_in_b, sinks,
              kv_norm, w_kv, w_out, norm_pre_mix, norm_post_mix, norm_pre_mlp, norm_post_mlp,
              w_up, w_down):
    Bp = x_prompt.shape[0]
    mk_list, mv_list = [], []
    for l in range(DEPTH):
        kv = rms_norm(mem_prompt, mem_norm[l]) @ w_mem_kv[l]
        mk_list.append(kv[..., :XATTN_WIDTH].reshape(Bp, N_MEM, XATTN_HEADS, XATTN_HEAD_DIM))
        mv_list.append(kv[..., XATTN_WIDTH:].reshape(Bp, N_MEM, XATTN_HEADS, XATTN_HEAD_DIM))
    mem_k_prompt = jnp.stack(mk_list)
    mem_v_prompt = jnp.stack(mv_list)

    hgrn0 = jnp.zeros((N_A_LAYERS, Bp, HGRN_HEADS, HGRN_HEAD_DIM, HGRN_HEAD_DIM), x_prompt.dtype)
    swa0 = jnp.zeros((Bp, WINDOW, SWA_KV_HEADS, SWA_HEAD_DIM), x_prompt.dtype)
    weights = (w_in_a, lb_logits, hgrn_norm, w_in_b, sinks, kv_norm, w_kv, w_out,
               norm_pre_mix, norm_post_mix, norm_pre_mlp, norm_post_mlp, w_up, w_down)

    y_prompt, hgrn_prompt, swa_k_prompt, swa_v_prompt = trunk(
        x_prompt, 0, mem_k_prompt, mem_v_prompt, hgrn0, swa0, swa0, *weights)
    y_sample, hgrn_sample, swa_k_sample, swa_v_sample = trunk(
        x_sample, PAST_LEN, cache_mem_k, cache_mem_v, state_hgrn, state_swa_k, state_swa_v, *weights)
    return (y_prompt, y_sample, mem_k_prompt, mem_v_prompt, hgrn_prompt, swa_k_prompt, swa_v_prompt,
            hgrn_sample, swa_k_sample, swa_v_sample)
```

```python
import functools

import jax
import jax.numpy as jnp
import numpy as np
from jax import lax
from jax.experimental import pallas as pl
from jax.experimental.pallas import tpu as pltpu

F32 = jnp.float32
BF16 = jnp.bfloat16

D_MODEL = 1024
DEPTH = 4
N_A_LAYERS = 2
N_MEM = 256
XATTN_HEADS = 4
XATTN_WIDTH = 256
HEAD_DIM = 64
HGRN_HEADS = 6
HGRN_DIM = 128
HGRN_WIDTH = HGRN_HEADS * HGRN_DIM
SWA_KV_HEADS = 4
SWA_GROUP = 3
SWA_WIDTH = 768
SWA_KV_WIDTH = 256
WINDOW = 128
PAST_LEN = 8192
ROPE_THETA = 10000.0
D_FF = 4096
EPS = 1e-6
ATTN_SCALE = HEAD_DIM ** -0.5

V7X_VMEM_BYTES = 64 * 1024 * 1024
VMEM_LIMIT_BYTES = V7X_VMEM_BYTES - 8 * 1024 * 1024
NEG_BIG = -1e30

NT_DIMS = (((1,), (1,)), ((), ()))


def _params(*semantics):
    return pltpu.CompilerParams(dimension_semantics=semantics, vmem_limit_bytes=VMEM_LIMIT_BYTES)


def _rms(x, gain):
    return x * lax.rsqrt(jnp.mean(x * x, axis=-1, keepdims=True) + EPS) * gain


def _dot(a, b):
    return jnp.dot(a, b, preferred_element_type=F32)


def _dot_nt(a, b):
    return lax.dot_general(a, b, NT_DIMS, preferred_element_type=F32)


def _sigmoid_pair(z):
    e = jnp.exp(-jnp.abs(z))
    r = 1.0 / (1.0 + e)
    er = e * r
    pos = z >= 0
    return jnp.where(pos, r, er), jnp.where(pos, er, r)


def _silu(x):
    return x * _sigmoid_pair(x)[0]


def _const_spec(shape):
    nd = len(shape)
    return pl.BlockSpec(shape, lambda *_: (0,) * nd)


def _norm_matmul_kernel(x_ref, g_ref, w_ref, o_ref):
    y = _rms(x_ref[...], g_ref[...])
    o_ref[...] = _dot(y.astype(BF16), w_ref[...])


def norm_matmul(x, gain, w, tm):
    m, d = x.shape
    n = w.shape[1]
    return pl.pallas_call(
        _norm_matmul_kernel,
        grid=(m // tm,),
        in_specs=[pl.BlockSpec((tm, d), lambda i: (i, 0)), _const_spec((1, d)), _const_spec((d, n))],
        out_specs=pl.BlockSpec((tm, n), lambda i: (i, 0)),
        out_shape=jax.ShapeDtypeStruct((m, n), F32),
        compiler_params=_params("parallel"),
        name="norm_matmul",
    )(x, gain.reshape(1, d), w)


def _mem_kv_kernel(x_ref, g_ref, w_ref, o_ref):
    y = _rms(x_ref[...], g_ref[0])
    o_ref[0] = _dot(y.astype(BF16), w_ref[0])


def mem_kv(mem, gains, w):
    m, d = mem.shape
    depth, _, n = w.shape
    return pl.pallas_call(
        _mem_kv_kernel,
        grid=(depth,),
        in_specs=[_const_spec((m, d)), pl.BlockSpec((1, 1, d), lambda l: (l, 0, 0)),
                  pl.BlockSpec((1, d, n), lambda l: (l, 0, 0))],
        out_specs=pl.BlockSpec((1, m, n), lambda l: (l, 0, 0)),
        out_shape=jax.ShapeDtypeStruct((depth, m, n), F32),
        compiler_params=_params("parallel"),
        name="mem_kv",
    )(mem, gains.reshape(depth, 1, d), w)


def _kv_proj_kernel(x_ref, g_ref, w_ref, cos_ref, sin_ref, k_ref, v_ref):
    y = _rms(x_ref[...], g_ref[...])
    acc = _dot(y.astype(BF16), w_ref[...])
    kw = SWA_KV_WIDTH
    k_ref[...] = acc[:, :kw] * cos_ref[...] + acc[:, kw:2 * kw] * sin_ref[...]
    v_ref[...] = acc[:, 2 * kw:]


def kv_proj(x, gain, w, cos, sin, tm):
    m, d = x.shape
    n = w.shape[1]
    kw = SWA_KV_WIDTH
    n_pos_tiles = cos.shape[0] // tm
    tab = pl.BlockSpec((tm, kw), lambda i: (i % n_pos_tiles, 0))
    out = pl.BlockSpec((tm, kw), lambda i: (i, 0))
    return pl.pallas_call(
        _kv_proj_kernel,
        grid=(m // tm,),
        in_specs=[pl.BlockSpec((tm, d), lambda i: (i, 0)), _const_spec((1, d)), _const_spec((d, n)), tab, tab],
        out_specs=[out, out],
        out_shape=[jax.ShapeDtypeStruct((m, kw), F32)] * 2,
        compiler_params=_params("parallel"),
        name="kv_proj",
    )(x, gain.reshape(1, d), w, cos, sin)


FF_CHUNK = 1024


def _post_mix_kernel(tok_ref, xo_ref, h_ref, wo_ref, g1_ref, g2_ref, wup_ref, wdn_ref, g3_ref, o_ref):
    tw = tok_ref.shape[1]
    mix = _dot(tok_ref[...].astype(BF16), wo_ref[0:tw, :]) + _dot(xo_ref[...].astype(BF16), wo_ref[tw:, :])
    h1 = h_ref[...] + _rms(mix, g1_ref[...])
    hn = _rms(h1, g2_ref[...]).astype(BF16)
    acc = jnp.zeros(h1.shape, F32)
    for c in range(D_FF // FF_CHUNK):
        cols = slice(c * FF_CHUNK, (c + 1) * FF_CHUNK)
        u = jnp.maximum(_dot(hn, wup_ref[:, cols]), 0.0)
        acc = acc + _dot((u * u).astype(BF16), wdn_ref[cols, :])
    o_ref[...] = h1 + _rms(acc, g3_ref[...])


def post_mix(tok, xo, h, wo, g_post_mix, g_pre_mlp, wup, wdn, g_post_mlp, tm):
    m, d = h.shape
    row = lambda w: pl.BlockSpec((tm, w), lambda i: (i, 0))
    single = dict(pipeline_mode=pl.Buffered(1))
    wspec = lambda shape: pl.BlockSpec(shape, lambda i: (0, 0), **single)
    return pl.pallas_call(
        _post_mix_kernel,
        grid=(m // tm,),
        in_specs=[row(tok.shape[1]), row(xo.shape[1]), row(d), wspec(wo.shape), _const_spec((1, d)),
                  _const_spec((1, d)), wspec(wup.shape), wspec(wdn.shape), _const_spec((1, d))],
        out_specs=row(d),
        out_shape=jax.ShapeDtypeStruct((m, d), F32),
        compiler_params=_params("parallel"),
        name="post_mix",
    )(tok, xo, h, wo, g_post_mix.reshape(1, d), g_pre_mlp.reshape(1, d), wup, wdn, g_post_mlp.reshape(1, d))


HGRN_CHUNK = 128
HGRN_LEVELS = (64, 32, 16, 8, 4, 2, 1)


def _layer_lower_bound(logits, layer):
    m = jnp.max(logits, axis=0, keepdims=True)
    e = jnp.exp(logits - m)
    p = e / jnp.sum(e, axis=0, keepdims=True)
    lb = jnp.zeros(p.shape[1:], F32)
    for j in range(1, layer + 1):
        lb = lb + p[j]
    return lb


def _split3(x):
    a = x.astype(BF16)
    r = x - a.astype(F32)
    b = r.astype(BF16)
    c = (r - b.astype(F32)).astype(BF16)
    return a, b, c


def _level_ref(b, w):
    n = b.shape[0]
    if w >= 4:
        pieces = [jnp.broadcast_to(b[s + w - 1:s + w, :], (2 * w, b.shape[1])) for s in range(0, n, 2 * w)]
        return jnp.concatenate(pieces, axis=0)
    sub = lax.broadcasted_iota(jnp.int32, (8, b.shape[1]), 0)
    pieces = []
    for s in range(0, n, 8):
        blk = None
        for t in range(0, 8, 2 * w):
            cand = jnp.broadcast_to(b[s + t + w - 1:s + t + w, :], (8, b.shape[1]))
            blk = cand if blk is None else jnp.where(sub >= t, cand, blk)
        pieces.append(blk)
    return jnp.concatenate(pieces, axis=0)


def _hgrn_chunk(qraw, z, v, gate, lb, gain, st):
    c = HGRN_CHUNK
    sig_p, sig_n = _sigmoid_pair(z)
    logf = jnp.log(lb + (1.0 - lb) * sig_p)
    k = (1.0 - lb) * sig_n
    q = _silu(qraw)

    ti = lax.broadcasted_iota(jnp.int32, (c, c), 0)
    si = lax.broadcasted_iota(jnp.int32, (c, c), 1)
    tril = (si <= ti).astype(BF16)
    b = sum(_dot(tril, part) for part in _split3(logf))

    row = lax.broadcasted_iota(jnp.int32, (c, HGRN_DIM), 0)
    txs = ti ^ si
    scores = jnp.where(txs == 0, _dot_nt(q.astype(BF16), k.astype(BF16)), 0.0)
    for w in HGRN_LEVELS:
        upper = (row & w) != 0
        d = b - _level_ref(b, w)
        e = jnp.exp(jnp.where(upper, d, -d))
        a_w = jnp.where(upper, q * e, 0.0).astype(BF16)
        b_w = jnp.where(upper, 0.0, k * e).astype(BF16)
        s_w = _dot_nt(a_w, b_w)
        scores = scores + (s_w if 2 * w == c else jnp.where(txs < 2 * w, s_w, 0.0))

    b_last = b[c - 1:c, :]
    o = _dot_nt((q * jnp.exp(b)).astype(BF16), st.astype(BF16)) + _dot(scores.astype(BF16), v.astype(BF16))
    ke = (k * jnp.exp(b_last - b)).astype(BF16)
    st_new = st * jnp.exp(b_last) + _dot(v.T.astype(BF16), ke)
    out = _rms(o, gain) * _silu(gate)
    return out, st_new


def _hgrn_kernel(q_ref, f_ref, i_ref, g_ref, lbl_ref, gain_ref, o_ref, sfin_ref, st_ref, *, layer, n_chunks):
    step = pl.program_id(2)

    @pl.when(step == 0)
    def _():
        st_ref[...] = jnp.zeros(st_ref.shape, F32)

    lb = _layer_lower_bound(lbl_ref[:, 0], layer)
    gain = gain_ref[0]

    def body(ci, carry):
        rows = pl.ds(pl.multiple_of(ci * HGRN_CHUNK, HGRN_CHUNK), HGRN_CHUNK)
        out, st_new = _hgrn_chunk(q_ref[rows, :], f_ref[rows, :], i_ref[rows, :], g_ref[rows, :], lb, gain,
                                  st_ref[...])
        o_ref[rows, :] = out
        st_ref[...] = st_new
        return carry

    lax.fori_loop(0, n_chunks, body, 0)

    @pl.when(step == pl.num_programs(2) - 1)
    def _():
        sfin_ref[0, 0] = st_ref[...].T


def hgrn_prompt(proj, lb_logits, gain, layer, batch, seq, tc):
    nct = seq // tc
    h, dh = HGRN_HEADS, HGRN_DIM
    sec = lambda s: pl.BlockSpec((tc, dh), lambda b, hh, c: (b * nct + c, s * h + hh))
    n_layers = lb_logits.shape[0]
    kern = functools.partial(_hgrn_kernel, layer=layer, n_chunks=tc // HGRN_CHUNK)
    return pl.pallas_call(
        kern,
        grid=(batch, h, nct),
        in_specs=[sec(0), sec(1), sec(2), sec(3),
                  pl.BlockSpec((n_layers, 1, 1, dh), lambda b, hh, c: (0, hh, 0, 0)),
                  pl.BlockSpec((1, 1, dh), lambda b, hh, c: (hh, 0, 0))],
        out_specs=[pl.BlockSpec((tc, dh), lambda b, hh, c: (b * nct + c, hh)),
                   pl.BlockSpec((1, 1, dh, dh), lambda b, hh, c: (b, hh, 0, 0))],
        out_shape=[jax.ShapeDtypeStruct((batch * seq, h * dh), F32),
                   jax.ShapeDtypeStruct((batch, h, dh, dh), F32)],
        scratch_shapes=[pltpu.VMEM((dh, dh), F32)],
        compiler_params=_params("parallel", "parallel", "arbitrary"),
        name="hgrn_prompt",
    )(proj, proj, proj, proj, lb_logits.reshape(n_layers, h, 1, dh), gain.reshape(h, 1, dh))


DEC_GROUP = 8


def _hgrn_decode_kernel(qc_ref, zc_ref, i_ref, g_ref, lblc_ref, gain_ref, s_ref, o_ref, snew_ref, *, layer,
                        n_groups):
    lb = _layer_lower_bound(lblc_ref[:, 0], layer)
    gain = gain_ref[0]
    dh = HGRN_DIM

    def body(gi, carry):
        sig_p, sig_n = _sigmoid_pair(zc_ref[gi, 0])
        f = lb + (1.0 - lb) * sig_p
        k = (1.0 - lb) * sig_n
        q = _silu(qc_ref[gi, 0])
        rows = []
        for j in range(DEC_GROUP):
            r = gi * DEC_GROUP + j
            v_row = i_ref[pl.ds(r, 1), :]
            s_new = (jnp.broadcast_to(f[:, j:j + 1], (dh, dh)) * s_ref[r, 0]
                     + jnp.broadcast_to(k[:, j:j + 1], (dh, dh)) * v_row)
            snew_ref[r, 0] = s_new
            rows.append(jnp.sum(jnp.broadcast_to(q[:, j:j + 1], (dh, dh)) * s_new, axis=0, keepdims=True))
        o = jnp.concatenate(rows, axis=0)
        grp = pl.ds(pl.multiple_of(gi * DEC_GROUP, DEC_GROUP), DEC_GROUP)
        o_ref[grp, :] = _rms(o, gain) * _silu(g_ref[grp, :])
        return carry

    lax.fori_loop(0, n_groups, body, 0)


def hgrn_decode(proj, q_cols, z_cols, lb_logits, gain, state, layer, bb):
    nb = state.shape[0]
    h, dh = HGRN_HEADS, HGRN_DIM
    n_layers = lb_logits.shape[0]
    n_groups = bb // DEC_GROUP
    col = pl.BlockSpec((n_groups, 1, dh, DEC_GROUP), lambda b, hh: (b, hh, 0, 0))
    sec = lambda s: pl.BlockSpec((bb, dh), lambda b, hh: (b, s * h + hh))
    st = pl.BlockSpec((bb, 1, dh, dh), lambda b, hh: (b, hh, 0, 0))
    kern = functools.partial(_hgrn_decode_kernel, layer=layer, n_groups=n_groups)
    return pl.pallas_call(
        kern,
        grid=(nb // bb, h),
        in_specs=[col, col, sec(2), sec(3),
                  pl.BlockSpec((n_layers, 1, dh, 1), lambda b, hh: (0, hh, 0, 0)),
                  pl.BlockSpec((1, 1, dh), lambda b, hh: (hh, 0, 0)), st],
        out_specs=[pl.BlockSpec((bb, dh), lambda b, hh: (b, hh)), st],
        out_shape=[jax.ShapeDtypeStruct((nb, h * dh), F32), jax.ShapeDtypeStruct(state.shape, F32)],
        compiler_params=_params("parallel", "parallel"),
        name="hgrn_decode",
    )(q_cols, z_cols, proj, proj, lb_logits.reshape(n_layers, h, dh, 1), gain.reshape(h, 1, dh), state)


def _lane_head(shape):
    return lax.broadcasted_iota(jnp.int32, shape, 1) // HEAD_DIM


def _mem_attn_kernel(q_ref, mk_ref, mv_ref, o_ref):
    q = q_ref[...] * ATTN_SCALE
    tl = q.shape[0]
    head = _lane_head(q.shape)
    qs = jnp.concatenate([jnp.where(head == hh, q, 0.0) for hh in range(XATTN_HEADS)], axis=0).astype(BF16)
    s = _dot_nt(qs, mk_ref[0])
    p = jnp.exp(s - jnp.max(s, axis=-1, keepdims=True))
    p = (p / jnp.sum(p, axis=-1, keepdims=True)).astype(BF16)
    o = _dot(p, mv_ref[0])
    out = jnp.zeros(q.shape, F32)
    for hh in range(XATTN_HEADS):
        out = jnp.where(head == hh, o[hh * tl:(hh + 1) * tl, :], out)
    o_ref[...] = out


def mem_attn_prompt(proj, col_block, mk, mv, batch, seq, tl):
    nlt = seq // tl
    w = XATTN_WIDTH
    mem = pl.BlockSpec((1, N_MEM, w), lambda i: (i // nlt, 0, 0))
    return pl.pallas_call(
        _mem_attn_kernel,
        grid=(batch * nlt,),
        in_specs=[pl.BlockSpec((tl, w), lambda i: (i, col_block)), mem, mem],
        out_specs=pl.BlockSpec((tl, w), lambda i: (i, 0)),
        out_shape=jax.ShapeDtypeStruct((batch * seq, w), F32),
        compiler_params=_params("parallel"),
        name="mem_attn_prompt",
    )(proj, mk, mv)


def _swa_kernel(sink_ref, pq_ref, pqr_ref, cos_ref, sin_ref, kp_ref, kc_ref, vp_ref, vc_ref, o_ref):
    nblk = pl.program_id(1)
    qb = WINDOW
    kw = SWA_KV_WIDTH
    cos = jnp.concatenate([cos_ref[...]] * SWA_GROUP, axis=1)
    sin = jnp.concatenate([sin_ref[...]] * SWA_GROUP, axis=1)
    q = (pq_ref[...] * cos + pqr_ref[...] * sin) * ATTN_SCALE
    keys = jnp.concatenate([kp_ref[0], kc_ref[0]], axis=0).astype(BF16)
    vals = jnp.concatenate([vp_ref[0], vc_ref[0]], axis=0).astype(BF16)

    head = _lane_head((qb, kw))
    combos = [(g, kvh) for g in range(SWA_GROUP) for kvh in range(SWA_KV_HEADS)]
    qs = jnp.concatenate([jnp.where(head == kvh, q[:, g * kw:(g + 1) * kw], 0.0) for g, kvh in combos],
                         axis=0).astype(BF16)
    s = _dot_nt(qs, keys)

    i_q = lax.broadcasted_iota(jnp.int32, (qb, 2 * qb), 0)
    j_k = lax.broadcasted_iota(jnp.int32, (qb, 2 * qb), 1)
    rel = qb + i_q - j_k
    valid = (rel >= 0) & (rel < WINDOW) & (j_k + nblk * qb >= qb)
    outs = []
    for g in range(SWA_GROUP):
        acc = jnp.zeros((qb, kw), F32)
        for kvh in range(SWA_KV_HEADS):
            idx = g * SWA_KV_HEADS + kvh
            sink = sink_ref[kvh * SWA_GROUP + g]
            sc = jnp.where(valid, s[idx * qb:(idx + 1) * qb, :], NEG_BIG)
            m = jnp.maximum(jnp.max(sc, axis=-1, keepdims=True), sink)
            p = jnp.exp(sc - m)
            den = jnp.sum(p, axis=-1, keepdims=True) + jnp.exp(sink - m)
            o = _dot((p / den).astype(BF16), vals)
            acc = jnp.where(head == kvh, o, acc)
        outs.append(acc)
    o_ref[...] = jnp.concatenate(outs, axis=1)


def swa_prompt(proj, k_all, v_all, cos, sin, sinks, batch, seq):
    qb = WINDOW
    nb = seq // qb
    kw = SWA_KV_WIDTH
    qspec = lambda c: pl.BlockSpec((qb, SWA_WIDTH), lambda b, n: (b * nb + n, c))
    tab = pl.BlockSpec((qb, kw), lambda b, n: (n, 0))
    prev = pl.BlockSpec((1, qb, kw), lambda b, n: (b, n, 0))
    cur = pl.BlockSpec((1, qb, kw), lambda b, n: (b, n + 1, 0))
    return pl.pallas_call(
        _swa_kernel,
        grid=(batch, nb),
        in_specs=[pl.BlockSpec(memory_space=pltpu.SMEM), qspec(0), qspec(1), tab, tab, prev, cur, prev, cur],
        out_specs=pl.BlockSpec((qb, SWA_WIDTH), lambda b, n: (b * nb + n, 0)),
        out_shape=jax.ShapeDtypeStruct((batch * seq, SWA_WIDTH), F32),
        compiler_params=_params("parallel", "parallel"),
        name="swa_prompt",
    )(sinks, proj, proj, cos, sin, k_all, k_all, v_all, v_all)


def _decode_attn_kernel(*refs, rope, use_sink, bb):
    if rope:
        qp_ref, qr_ref, cos_ref, sin_ref, sink_ref, k_ref, v_ref, o_ref = refs
    else:
        qp_ref, k_ref, v_ref, o_ref = refs

    def body(bi, carry):
        q = qp_ref[bi]
        if rope:
            q = q * cos_ref[...] + qr_ref[bi] * sin_ref[...]
        q = (q * ATTN_SCALE).astype(BF16)
        s = _dot_nt(q, k_ref[bi].astype(BF16))
        m = jnp.max(s, axis=-1, keepdims=True)
        if use_sink:
            sink = sink_ref[:, 0:1]
            m = jnp.maximum(m, sink)
        p = jnp.exp(s - m)
        den = jnp.sum(p, axis=-1, keepdims=True)
        if use_sink:
            den = den + jnp.exp(sink - m)
        o_ref[bi] = _dot((p / den).astype(BF16), v_ref[bi].astype(BF16))
        return carry

    lax.fori_loop(0, bb, body, 0)


def decode_attn(q_rows, k, v, bb, q_rot=None, cos=None, sin=None, sink_rows=None):
    nb, r, c = q_rows.shape
    m = k.shape[1]
    rope = q_rot is not None
    qspec = pl.BlockSpec((bb, r, c), lambda i: (i, 0, 0))
    kvspec = pl.BlockSpec((bb, m, c), lambda i: (i, 0, 0))
    if rope:
        ins = [qspec, qspec, _const_spec((1, c)), _const_spec((1, c)), _const_spec(sink_rows.shape), kvspec, kvspec]
        args = (q_rows, q_rot, cos, sin, sink_rows, k, v)
    else:
        ins = [qspec, kvspec, kvspec]
        args = (q_rows, k, v)
    kern = functools.partial(_decode_attn_kernel, rope=rope, use_sink=rope, bb=bb)
    return pl.pallas_call(
        kern,
        grid=(nb // bb,),
        in_specs=ins,
        out_specs=qspec,
        out_shape=jax.ShapeDtypeStruct((nb, r, c), F32),
        compiler_params=_params("parallel"),
        name="decode_attn_swa" if rope else "decode_attn_mem",
    )(*args)


def _rot_cols(w):
    d_in, n = w.shape
    w4 = w.reshape(d_in, n // HEAD_DIM, 2, HEAD_DIM // 2)
    return jnp.concatenate([-w4[:, :, 1:2], w4[:, :, 0:1]], axis=2).reshape(d_in, n)


def _rope_tables(positions, n_heads):
    half = HEAD_DIM // 2
    inv = ROPE_THETA ** (-jnp.arange(half, dtype=F32) / half)
    ang = positions.astype(F32)[:, None] * inv[None, :]
    cos = jnp.tile(jnp.cos(ang), (1, 2 * n_heads))
    sin = jnp.tile(jnp.sin(ang), (1, 2 * n_heads))
    return cos, sin


_SWA_PERM = np.array([(kvh * SWA_GROUP + g) * HEAD_DIM + d
                      for g in range(SWA_GROUP) for kvh in range(SWA_KV_HEADS) for d in range(HEAD_DIM)])

TM_ROWS = 512
TM_POST = 256
HGRN_TC = 512
MEM_TL = 512
DEC_BB = 16


def _head_rows(x, n_rows):
    nb = x.shape[0]
    g = x.shape[1] // XATTN_WIDTH
    x3 = x.reshape(nb, g, 1, XATTN_WIDTH)
    hb = (jnp.arange(XATTN_WIDTH) // HEAD_DIM)[None, None, None, :]
    rows = jnp.where(hb == jnp.arange(4)[None, None, :, None], x3, 0.0).reshape(nb, g * 4, XATTN_WIDTH)
    return jnp.pad(rows, ((0, 0), (0, n_rows - g * 4), (0, 0)))


def _head_rows_merge(o, g):
    nb = o.shape[0]
    o4 = o[:, :g * 4].reshape(nb, g, 4, XATTN_WIDTH)
    hb = (jnp.arange(XATTN_WIDTH) // HEAD_DIM)[None, None, None, :]
    return jnp.sum(jnp.where(hb == jnp.arange(4)[None, None, :, None], o4, 0.0), axis=2).reshape(nb, g * XATTN_WIDTH)


def kernel(x_prompt, x_sample, cache_mem_k, cache_mem_v, state_hgrn, state_swa_k, state_swa_v, mem_prompt,
           mem_norm, w_mem_kv, w_in_a, lb_logits, hgrn_norm, w_in_b, sinks, kv_norm, w_kv, w_out,
           norm_pre_mix, norm_post_mix, norm_pre_mlp, norm_post_mlp, w_up, w_down):
    bp, seq, d = x_prompt.shape
    bd = x_sample.shape[0]
    n_b = DEPTH - N_A_LAYERS

    w_in_a16 = w_in_a.astype(BF16)
    wq = w_in_b[:, :, :SWA_WIDTH]
    w_in_b16 = jnp.stack([
        jnp.concatenate([wq[j][:, _SWA_PERM], _rot_cols(wq[j])[:, _SWA_PERM], w_in_b[j][:, SWA_WIDTH:]], axis=1)
        for j in range(n_b)]).astype(BF16)
    wk = w_kv[:, :SWA_KV_WIDTH]
    w_kv16 = jnp.concatenate([wk, _rot_cols(wk), w_kv[:, SWA_KV_WIDTH:]], axis=1).astype(BF16)
    w_out16 = jnp.stack([w_out[l] if l < N_A_LAYERS else
                         jnp.concatenate([w_out[l][:SWA_WIDTH][_SWA_PERM], w_out[l][SWA_WIDTH:]], axis=0)
                         for l in range(DEPTH)]).astype(BF16)
    w_up16 = w_up.astype(BF16)
    w_down16 = w_down.astype(BF16)
    sinks_g = sinks.astype(F32)

    cos_p, sin_p = _rope_tables(jnp.arange(seq), SWA_KV_HEADS)
    cos_d, sin_d = _rope_tables(jnp.full((1,), PAST_LEN), SWA_KV_HEADS)

    mem_rows = mem_prompt.reshape(bp * N_MEM, d)
    mkv = mem_kv(mem_rows, mem_norm, w_mem_kv.astype(BF16))
    mem_k_prompt = mkv[:, :, :XATTN_WIDTH].reshape(DEPTH, bp, N_MEM, XATTN_HEADS, HEAD_DIM)
    mem_v_prompt = mkv[:, :, XATTN_WIDTH:].reshape(DEPTH, bp, N_MEM, XATTN_HEADS, HEAD_DIM)
    mk16 = mkv[:, :, :XATTN_WIDTH].reshape(DEPTH, bp, N_MEM, XATTN_WIDTH).astype(BF16)
    mv16 = mkv[:, :, XATTN_WIDTH:].reshape(DEPTH, bp, N_MEM, XATTN_WIDTH).astype(BF16)

    h = x_prompt.reshape(bp * seq, d)
    hgrn_prompt_states = []
    k_all = v_all = None
    for l in range(DEPTH):
        if l == N_A_LAYERS:
            k_new, v_new = kv_proj(h, kv_norm, w_kv16, cos_p, sin_p, TM_ROWS)
            zeros = jnp.zeros((bp, WINDOW, SWA_KV_WIDTH), F32)
            k_all = jnp.concatenate([zeros, k_new.reshape(bp, seq, SWA_KV_WIDTH)], axis=1)
            v_all = jnp.concatenate([zeros, v_new.reshape(bp, seq, SWA_KV_WIDTH)], axis=1)
        if l < N_A_LAYERS:
            proj = norm_matmul(h, norm_pre_mix[l], w_in_a16[l], TM_ROWS)
            tok, s_fin = hgrn_prompt(proj, lb_logits, hgrn_norm[l], l, bp, seq, HGRN_TC)
            hgrn_prompt_states.append(s_fin)
            cq_block = 4 * HGRN_WIDTH // XATTN_WIDTH
        else:
            j = l - N_A_LAYERS
            proj = norm_matmul(h, norm_pre_mix[l], w_in_b16[j], TM_ROWS)
            tok = swa_prompt(proj, k_all, v_all, cos_p, sin_p, sinks_g[j], bp, seq)
            cq_block = 2 * SWA_WIDTH // XATTN_WIDTH
        xo = mem_attn_prompt(proj, cq_block, mk16[l], mv16[l], bp, seq, MEM_TL)
        h = post_mix(tok, xo, h, w_out16[l], norm_post_mix[l], norm_pre_mlp[l], w_up16[l], w_down16[l],
                     norm_post_mlp[l], TM_POST)
    y_prompt = h.reshape(bp, seq, d)
    hgrn_prompt_out = jnp.stack(hgrn_prompt_states)
    swa_k_prompt = k_all[:, -WINDOW:].reshape(bp, WINDOW, SWA_KV_HEADS, HEAD_DIM)
    swa_v_prompt = v_all[:, -WINDOW:].reshape(bp, WINDOW, SWA_KV_HEADS, HEAD_DIM)

    h = x_sample.reshape(bd, d)
    hgrn_sample_states = []
    swa_k_sample = swa_v_sample = None
    sink_rows = None
    for l in range(DEPTH):
        if l == N_A_LAYERS:
            k_new, v_new = kv_proj(h, kv_norm, w_kv16, jnp.tile(cos_d, (bd, 1)), jnp.tile(sin_d, (bd, 1)), bd)
            swa_k_sample = jnp.concatenate(
                [state_swa_k[:, 1:], k_new.reshape(bd, 1, SWA_KV_HEADS, HEAD_DIM)], axis=1)
            swa_v_sample = jnp.concatenate(
                [state_swa_v[:, 1:], v_new.reshape(bd, 1, SWA_KV_HEADS, HEAD_DIM)], axis=1)
        if l < N_A_LAYERS:
            proj = norm_matmul(h, norm_pre_mix[l], w_in_a16[l], bd)
            cols = lambda x: x.reshape(bd // DEC_GROUP, DEC_GROUP, HGRN_HEADS, HGRN_DIM).transpose(0, 2, 3, 1)
            tok, s_new = hgrn_decode(proj, cols(proj[:, :HGRN_WIDTH]), cols(proj[:, HGRN_WIDTH:2 * HGRN_WIDTH]),
                                     lb_logits, hgrn_norm[l], state_hgrn[l], l, DEC_BB)
            hgrn_sample_states.append(s_new)
            cq = proj[:, 4 * HGRN_WIDTH:]
        else:
            j = l - N_A_LAYERS
            proj = norm_matmul(h, norm_pre_mix[l], w_in_b16[j], bd)
            rows = SWA_GROUP * SWA_KV_HEADS + 4
            sink_gk = sinks_g[j].reshape(SWA_KV_HEADS, SWA_GROUP).T.reshape(-1)
            sink_rows = jnp.broadcast_to(jnp.pad(sink_gk, (0, 4))[:, None], (rows, 128))
            o = decode_attn(_head_rows(proj[:, :SWA_WIDTH], rows),
                            swa_k_sample.reshape(bd, WINDOW, SWA_KV_WIDTH),
                            swa_v_sample.reshape(bd, WINDOW, SWA_KV_WIDTH), DEC_BB,
                            q_rot=_head_rows(proj[:, SWA_WIDTH:2 * SWA_WIDTH], rows),
                            cos=cos_d, sin=sin_d, sink_rows=sink_rows)
            tok = _head_rows_merge(o, SWA_GROUP)
            cq = proj[:, 2 * SWA_WIDTH:]
        o = decode_attn(_head_rows(cq, 8), cache_mem_k[l].reshape(bd, N_MEM, XATTN_WIDTH),
                        cache_mem_v[l].reshape(bd, N_MEM, XATTN_WIDTH), DEC_BB // 2)
        xo = _head_rows_merge(o, 1)
        h = post_mix(tok, xo, h, w_out16[l], norm_post_mix[l], norm_pre_mlp[l], w_up16[l], w_down16[l],
                     norm_post_mlp[l], bd)
    y_sample = h.reshape(bd, 1, d)

    return (y_prompt, y_sample, mem_k_prompt, mem_v_prompt, hgrn_prompt_out, swa_k_prompt, swa_v_prompt,
            jnp.stack(hgrn_sample_states), swa_k_sample, swa_v_sample)
```

```python
import functools

import jax
import jax.numpy as jnp
import numpy as np
from jax import lax
from jax.experimental import pallas as pl
from jax.experimental.pallas import tpu as pltpu

F32 = jnp.float32
BF16 = jnp.bfloat16

D_MODEL = 1024
DEPTH = 4
N_A_LAYERS = 2
N_MEM = 256
XATTN_HEADS = 4
XATTN_WIDTH = 256
HEAD_DIM = 64
HGRN_HEADS = 6
HGRN_DIM = 128
HGRN_WIDTH = HGRN_HEADS * HGRN_DIM
SWA_KV_HEADS = 4
SWA_GROUP = 3
SWA_WIDTH = 768
SWA_KV_WIDTH = 256
WINDOW = 128
PAST_LEN = 8192
ROPE_THETA = 10000.0
D_FF = 4096
EPS = 1e-6
ATTN_SCALE = HEAD_DIM ** -0.5

V7X_VMEM_BYTES = 64 * 1024 * 1024
VMEM_LIMIT_BYTES = V7X_VMEM_BYTES - 8 * 1024 * 1024
NEG_BIG = -1e30

NT_DIMS = (((1,), (1,)), ((), ()))


def _params(*semantics):
    return pltpu.CompilerParams(dimension_semantics=semantics, vmem_limit_bytes=VMEM_LIMIT_BYTES)


def _rms(x, gain):
    return x * lax.rsqrt(jnp.mean(x * x, axis=-1, keepdims=True) + EPS) * gain


def _dot(a, b):
    return jnp.dot(a, b, preferred_element_type=F32)


def _dot_nt(a, b):
    return lax.dot_general(a, b, NT_DIMS, preferred_element_type=F32)


def _sigmoid_pair(z):
    e = jnp.exp(-jnp.abs(z))
    r = 1.0 / (1.0 + e)
    er = e * r
    pos = z >= 0
    return jnp.where(pos, r, er), jnp.where(pos, er, r)


def _silu(x):
    return x * _sigmoid_pair(x)[0]


def _const_spec(shape):
    nd = len(shape)
    return pl.BlockSpec(shape, lambda *_: (0,) * nd)


def _norm_matmul_kernel(x_ref, g_ref, w_ref, o_ref):
    y = _rms(x_ref[...], g_ref[...])
    o_ref[...] = _dot(y.astype(BF16), w_ref[...])


def norm_matmul(x, gain, w, tm):
    m, d = x.shape
    n = w.shape[1]
    return pl.pallas_call(
        _norm_matmul_kernel,
        grid=(m // tm,),
        in_specs=[pl.BlockSpec((tm, d), lambda i: (i, 0)), _const_spec((1, d)), _const_spec((d, n))],
        out_specs=pl.BlockSpec((tm, n), lambda i: (i, 0)),
        out_shape=jax.ShapeDtypeStruct((m, n), F32),
        compiler_params=_params("parallel"),
        name="norm_matmul",
    )(x, gain.reshape(1, d), w)


def _mem_kv_kernel(x_ref, g_ref, w_ref, o_ref):
    y = _rms(x_ref[...], g_ref[0])
    o_ref[0] = _dot(y.astype(BF16), w_ref[0])


def mem_kv(mem, gains, w):
    m, d = mem.shape
    depth, _, n = w.shape
    return pl.pallas_call(
        _mem_kv_kernel,
        grid=(depth,),
        in_specs=[_const_spec((m, d)), pl.BlockSpec((1, 1, d), lambda l: (l, 0, 0)),
                  pl.BlockSpec((1, d, n), lambda l: (l, 0, 0))],
        out_specs=pl.BlockSpec((1, m, n), lambda l: (l, 0, 0)),
        out_shape=jax.ShapeDtypeStruct((depth, m, n), F32),
        compiler_params=_params("parallel"),
        name="mem_kv",
    )(mem, gains.reshape(depth, 1, d), w)


def _kv_proj_kernel(x_ref, g_ref, w_ref, cos_ref, sin_ref, k_ref, v_ref):
    y = _rms(x_ref[...], g_ref[...])
    acc = _dot(y.astype(BF16), w_ref[...])
    kw = SWA_KV_WIDTH
    k_ref[...] = acc[:, :kw] * cos_ref[...] + acc[:, kw:2 * kw] * sin_ref[...]
    v_ref[...] = acc[:, 2 * kw:]


def kv_proj(x, gain, w, cos, sin, tm):
    m, d = x.shape
    n = w.shape[1]
    kw = SWA_KV_WIDTH
    n_pos_tiles = cos.shape[0] // tm
    tab = pl.BlockSpec((tm, kw), lambda i: (i % n_pos_tiles, 0))
    out = pl.BlockSpec((tm, kw), lambda i: (i, 0))
    return pl.pallas_call(
        _kv_proj_kernel,
        grid=(m // tm,),
        in_specs=[pl.BlockSpec((tm, d), lambda i: (i, 0)), _const_spec((1, d)), _const_spec((d, n)), tab, tab],
        out_specs=[out, out],
        out_shape=[jax.ShapeDtypeStruct((m, kw), F32)] * 2,
        compiler_params=_params("parallel"),
        name="kv_proj",
    )(x, gain.reshape(1, d), w, cos, sin)


FF_CHUNK = 1024


def _post_mix_kernel(tok_ref, xo_ref, h_ref, wo_ref, g1_ref, g2_ref, wup_ref, wdn_ref, g3_ref, o_ref):
    tw = tok_ref.shape[1]
    mix = _dot(tok_ref[...].astype(BF16), wo_ref[0:tw, :]) + _dot(xo_ref[...].astype(BF16), wo_ref[tw:, :])
    h1 = h_ref[...] + _rms(mix, g1_ref[...])
    hn = _rms(h1, g2_ref[...]).astype(BF16)
    acc = jnp.zeros(h1.shape, F32)
    for c in range(D_FF // FF_CHUNK):
        cols = slice(c * FF_CHUNK, (c + 1) * FF_CHUNK)
        u = jnp.maximum(_dot(hn, wup_ref[:, cols]), 0.0)
        acc = acc + _dot((u * u).astype(BF16), wdn_ref[cols, :])
    o_ref[...] = h1 + _rms(acc, g3_ref[...])


def post_mix(tok, xo, h, wo, g_post_mix, g_pre_mlp, wup, wdn, g_post_mlp, tm):
    m, d = h.shape
    row = lambda w: pl.BlockSpec((tm, w), lambda i: (i, 0))
    single = dict(pipeline_mode=pl.Buffered(1))
    wspec = lambda shape: pl.BlockSpec(shape, lambda i: (0, 0), **single)
    return pl.pallas_call(
        _post_mix_kernel,
        grid=(m // tm,),
        in_specs=[row(tok.shape[1]), row(xo.shape[1]), row(d), wspec(wo.shape), _const_spec((1, d)),
                  _const_spec((1, d)), wspec(wup.shape), wspec(wdn.shape), _const_spec((1, d))],
        out_specs=row(d),
        out_shape=jax.ShapeDtypeStruct((m, d), F32),
        compiler_params=_params("parallel"),
        name="post_mix",
    )(tok, xo, h, wo, g_post_mix.reshape(1, d), g_pre_mlp.reshape(1, d), wup, wdn, g_post_mlp.reshape(1, d))


HGRN_CHUNK = 128
HGRN_LEVELS = (64, 32, 16, 8, 4, 2, 1)


def _layer_lower_bound(logits, layer):
    m = jnp.max(logits, axis=0, keepdims=True)
    e = jnp.exp(logits - m)
    p = e / jnp.sum(e, axis=0, keepdims=True)
    lb = jnp.zeros(p.shape[1:], F32)
    for j in range(1, layer + 1):
        lb = lb + p[j]
    return lb


def _split3(x):
    a = x.astype(BF16)
    r = x - a.astype(F32)
    b = r.astype(BF16)
    c = (r - b.astype(F32)).astype(BF16)
    return a, b, c


def _level_ref(b, w):
    n = b.shape[0]
    if w >= 4:
        pieces = [jnp.broadcast_to(b[s + w - 1:s + w, :], (2 * w, b.shape[1])) for s in range(0, n, 2 * w)]
        return jnp.concatenate(pieces, axis=0)
    sub = lax.broadcasted_iota(jnp.int32, (8, b.shape[1]), 0)
    pieces = []
    for s in range(0, n, 8):
        blk = None
        for t in range(0, 8, 2 * w):
            cand = jnp.broadcast_to(b[s + t + w - 1:s + t + w, :], (8, b.shape[1]))
            blk = cand if blk is None else jnp.where(sub >= t, cand, blk)
        pieces.append(blk)
    return jnp.concatenate(pieces, axis=0)


def _hgrn_chunk(qraw, z, v, gate, lb, gain, st):
    c = HGRN_CHUNK
    sig_p, sig_n = _sigmoid_pair(z)
    logf = jnp.log(lb + (1.0 - lb) * sig_p)
    k = (1.0 - lb) * sig_n
    q = _silu(qraw)

    ti = lax.broadcasted_iota(jnp.int32, (c, c), 0)
    si = lax.broadcasted_iota(jnp.int32, (c, c), 1)
    tril = (si <= ti).astype(BF16)
    b = sum(_dot(tril, part) for part in _split3(logf))

    row = lax.broadcasted_iota(jnp.int32, (c, HGRN_DIM), 0)
    txs = ti ^ si
    scores = jnp.where(txs == 0, _dot_nt(q.astype(BF16), k.astype(BF16)), 0.0)
    for w in HGRN_LEVELS:
        upper = (row & w) != 0
        d = b - _level_ref(b, w)
        e = jnp.exp(jnp.where(upper, d, -d))
        a_w = jnp.where(upper, q * e, 0.0).astype(BF16)
        b_w = jnp.where(upper, 0.0, k * e).astype(BF16)
        s_w = _dot_nt(a_w, b_w)
        scores = scores + (s_w if 2 * w == c else jnp.where(txs < 2 * w, s_w, 0.0))

    b_last = b[c - 1:c, :]
    o = _dot_nt((q * jnp.exp(b)).astype(BF16), st.astype(BF16)) + _dot(scores.astype(BF16), v.astype(BF16))
    ke = (k * jnp.exp(b_last - b)).astype(BF16)
    st_new = st * jnp.exp(b_last) + _dot(v.T.astype(BF16), ke)
    out = _rms(o, gain) * _silu(gate)
    return out, st_new


def _hgrn_kernel(q_ref, f_ref, i_ref, g_ref, lbl_ref, gain_ref, o_ref, sfin_ref, st_ref, *, layer, n_chunks,
                 heads):
    step = pl.program_id(2)
    dh = HGRN_DIM

    @pl.when(step == 0)
    def _():
        st_ref[...] = jnp.zeros(st_ref.shape, F32)

    lbs = [_layer_lower_bound(lbl_ref[:, hh], layer) for hh in range(heads)]
    gains = [gain_ref[hh] for hh in range(heads)]

    def body(ci, carry):
        rows = pl.ds(pl.multiple_of(ci * HGRN_CHUNK, HGRN_CHUNK), HGRN_CHUNK)
        for hh in range(heads):
            lanes = slice(hh * dh, (hh + 1) * dh)
            out, st_new = _hgrn_chunk(q_ref[rows, lanes], f_ref[rows, lanes], i_ref[rows, lanes],
                                      g_ref[rows, lanes], lbs[hh], gains[hh], st_ref[hh])
            o_ref[rows, lanes] = out
            st_ref[hh] = st_new
        return carry

    lax.fori_loop(0, n_chunks, body, 0)

    @pl.when(step == pl.num_programs(2) - 1)
    def _():
        for hh in range(heads):
            sfin_ref[0, hh] = st_ref[hh].T


def hgrn_prompt(proj, lb_logits, gain, layer, batch, seq, tc, heads):
    nct = seq // tc
    h, dh = HGRN_HEADS, HGRN_DIM
    nhb = h // heads
    sec = lambda s: pl.BlockSpec((tc, heads * dh), lambda b, hb, c: (b * nct + c, s * nhb + hb))
    n_layers = lb_logits.shape[0]
    kern = functools.partial(_hgrn_kernel, layer=layer, n_chunks=tc // HGRN_CHUNK, heads=heads)
    return pl.pallas_call(
        kern,
        grid=(batch, nhb, nct),
        in_specs=[sec(0), sec(1), sec(2), sec(3),
                  pl.BlockSpec((n_layers, heads, 1, dh), lambda b, hb, c: (0, hb, 0, 0)),
                  pl.BlockSpec((heads, 1, dh), lambda b, hb, c: (hb, 0, 0))],
        out_specs=[pl.BlockSpec((tc, heads * dh), lambda b, hb, c: (b * nct + c, hb)),
                   pl.BlockSpec((1, heads, dh, dh), lambda b, hb, c: (b, hb, 0, 0))],
        out_shape=[jax.ShapeDtypeStruct((batch * seq, h * dh), F32),
                   jax.ShapeDtypeStruct((batch, h, dh, dh), F32)],
        scratch_shapes=[pltpu.VMEM((heads, dh, dh), F32)],
        compiler_params=_params("parallel", "parallel", "arbitrary"),
        name="hgrn_prompt",
    )(proj, proj, proj, proj, lb_logits.reshape(n_layers, h, 1, dh), gain.reshape(h, 1, dh))


DEC_GROUP = 8


def _hgrn_decode_kernel(*refs, layer, n_groups):
    qc_ref, zc_ref, i_ref, g_ref, lblc_ref, gain_ref, s_ref = refs[:7]
    o_ref, snew_ref = refs[-2:]
    lb = _layer_lower_bound(lblc_ref[:, 0], layer)
    gain = gain_ref[0]
    dh = HGRN_DIM

    def body(gi, carry):
        sig_p, sig_n = _sigmoid_pair(zc_ref[gi, 0])
        f = lb + (1.0 - lb) * sig_p
        k = (1.0 - lb) * sig_n
        q = _silu(qc_ref[gi, 0])
        rows = []
        for j in range(DEC_GROUP):
            r = gi * DEC_GROUP + j
            v_row = i_ref[pl.ds(r, 1), :]
            s_new = (jnp.broadcast_to(f[:, j:j + 1], (dh, dh)) * s_ref[0, r, 0]
                     + jnp.broadcast_to(k[:, j:j + 1], (dh, dh)) * v_row)
            snew_ref[0, r, 0] = s_new
            rows.append(jnp.sum(jnp.broadcast_to(q[:, j:j + 1], (dh, dh)) * s_new, axis=0, keepdims=True))
        o = jnp.concatenate(rows, axis=0)
        grp = pl.ds(pl.multiple_of(gi * DEC_GROUP, DEC_GROUP), DEC_GROUP)
        o_ref[grp, :] = _rms(o, gain) * _silu(g_ref[grp, :])
        return carry

    lax.fori_loop(0, n_groups, body, 0)


def hgrn_decode(proj, q_cols, z_cols, lb_logits, gain, states, layer, bb, new_states=None):
    n_layers, nb = states.shape[:2]
    h, dh = HGRN_HEADS, HGRN_DIM
    n_groups = bb // DEC_GROUP
    col = pl.BlockSpec((n_groups, 1, dh, DEC_GROUP), lambda b, hh: (b, hh, 0, 0))
    sec = lambda s: pl.BlockSpec((bb, dh), lambda b, hh: (b, s * h + hh))
    st = pl.BlockSpec((1, bb, 1, dh, dh), lambda b, hh: (layer, b, hh, 0, 0))
    in_specs = [col, col, sec(2), sec(3),
                pl.BlockSpec((n_layers, 1, dh, 1), lambda b, hh: (0, hh, 0, 0)),
                pl.BlockSpec((1, 1, dh), lambda b, hh: (hh, 0, 0)), st]
    args = [q_cols, z_cols, proj, proj, lb_logits.reshape(n_layers, h, dh, 1), gain.reshape(h, 1, dh), states]
    aliases = {}
    if new_states is not None:
        in_specs.append(pl.BlockSpec(memory_space=pl.ANY))
        args.append(new_states)
        aliases = {len(args) - 1: 1}
    kern = functools.partial(_hgrn_decode_kernel, layer=layer, n_groups=n_groups)
    return pl.pallas_call(
        kern,
        grid=(nb // bb, h),
        in_specs=in_specs,
        out_specs=[pl.BlockSpec((bb, dh), lambda b, hh: (b, hh)), st],
        out_shape=[jax.ShapeDtypeStruct((nb, h * dh), F32), jax.ShapeDtypeStruct(states.shape, F32)],
        input_output_aliases=aliases,
        compiler_params=_params("parallel", "parallel"),
        name="hgrn_decode",
    )(*args)


def _lane_head(shape):
    return lax.broadcasted_iota(jnp.int32, shape, 1) // HEAD_DIM


def _mem_attn_kernel(q_ref, mk_ref, mv_ref, o_ref):
    q = q_ref[...] * ATTN_SCALE
    tl = q.shape[0]
    head = _lane_head(q.shape)
    qs = jnp.concatenate([jnp.where(head == hh, q, 0.0) for hh in range(XATTN_HEADS)], axis=0).astype(BF16)
    s = _dot_nt(qs, mk_ref[0])
    p = jnp.exp(s - jnp.max(s, axis=-1, keepdims=True))
    p = (p / jnp.sum(p, axis=-1, keepdims=True)).astype(BF16)
    o = _dot(p, mv_ref[0])
    out = jnp.zeros(q.shape, F32)
    for hh in range(XATTN_HEADS):
        out = jnp.where(head == hh, o[hh * tl:(hh + 1) * tl, :], out)
    o_ref[...] = out


def mem_attn_prompt(proj, col_block, mk, mv, batch, seq, tl):
    nlt = seq // tl
    w = XATTN_WIDTH
    mem = pl.BlockSpec((1, N_MEM, w), lambda i: (i // nlt, 0, 0))
    return pl.pallas_call(
        _mem_attn_kernel,
        grid=(batch * nlt,),
        in_specs=[pl.BlockSpec((tl, w), lambda i: (i, col_block)), mem, mem],
        out_specs=pl.BlockSpec((tl, w), lambda i: (i, 0)),
        out_shape=jax.ShapeDtypeStruct((batch * seq, w), F32),
        compiler_params=_params("parallel"),
        name="mem_attn_prompt",
    )(proj, mk, mv)


def _swa_kernel(sink_ref, pq_ref, pqr_ref, cos_ref, sin_ref, kp_ref, kc_ref, vp_ref, vc_ref, o_ref):
    nblk = pl.program_id(1)
    qb = WINDOW
    kw = SWA_KV_WIDTH
    cos = jnp.concatenate([cos_ref[...]] * SWA_GROUP, axis=1)
    sin = jnp.concatenate([sin_ref[...]] * SWA_GROUP, axis=1)
    q = (pq_ref[...] * cos + pqr_ref[...] * sin) * ATTN_SCALE
    keys = jnp.concatenate([kp_ref[0], kc_ref[0]], axis=0).astype(BF16)
    vals = jnp.concatenate([vp_ref[0], vc_ref[0]], axis=0).astype(BF16)

    head = _lane_head((qb, kw))
    combos = [(g, kvh) for g in range(SWA_GROUP) for kvh in range(SWA_KV_HEADS)]
    qs = jnp.concatenate([jnp.where(head == kvh, q[:, g * kw:(g + 1) * kw], 0.0) for g, kvh in combos],
                         axis=0).astype(BF16)
    s = _dot_nt(qs, keys)

    i_q = lax.broadcasted_iota(jnp.int32, (qb, 2 * qb), 0)
    j_k = lax.broadcasted_iota(jnp.int32, (qb, 2 * qb), 1)
    rel = qb + i_q - j_k
    valid = (rel >= 0) & (rel < WINDOW) & (j_k + nblk * qb >= qb)
    outs = []
    for g in range(SWA_GROUP):
        acc = jnp.zeros((qb, kw), F32)
        for kvh in range(SWA_KV_HEADS):
            idx = g * SWA_KV_HEADS + kvh
            sink = sink_ref[kvh * SWA_GROUP + g]
            sc = jnp.where(valid, s[idx * qb:(idx + 1) * qb, :], NEG_BIG)
            m = jnp.maximum(jnp.max(sc, axis=-1, keepdims=True), sink)
            p = jnp.exp(sc - m)
            den = jnp.sum(p, axis=-1, keepdims=True) + jnp.exp(sink - m)
            o = _dot((p / den).astype(BF16), vals)
            acc = jnp.where(head == kvh, o, acc)
        outs.append(acc)
    o_ref[...] = jnp.concatenate(outs, axis=1)


def swa_prompt(proj, k_new, v_new, cos, sin, sinks, batch, seq):
    qb = WINDOW
    nb = seq // qb
    kw = SWA_KV_WIDTH
    qspec = lambda c: pl.BlockSpec((qb, SWA_WIDTH), lambda b, n: (b * nb + n, c))
    tab = pl.BlockSpec((qb, kw), lambda b, n: (n, 0))
    prev = pl.BlockSpec((1, qb, kw), lambda b, n: (b, jnp.maximum(n - 1, 0), 0))
    cur = pl.BlockSpec((1, qb, kw), lambda b, n: (b, n, 0))
    return pl.pallas_call(
        _swa_kernel,
        grid=(batch, nb),
        in_specs=[pl.BlockSpec(memory_space=pltpu.SMEM), qspec(0), qspec(1), tab, tab, prev, cur, prev, cur],
        out_specs=pl.BlockSpec((qb, SWA_WIDTH), lambda b, n: (b * nb + n, 0)),
        out_shape=jax.ShapeDtypeStruct((batch * seq, SWA_WIDTH), F32),
        compiler_params=_params("parallel", "parallel"),
        name="swa_prompt",
    )(sinks, proj, proj, cos, sin, k_new, k_new, v_new, v_new)


def _col(x, j, width):
    return jnp.broadcast_to(x[:, j:j + 1], (x.shape[0], width))


def _decode_heads(q, kts, vts, sink):
    n_keys = kts[0].shape[1]
    s = jnp.concatenate([jnp.sum(_col(q, j, n_keys) * kts[j], axis=0, keepdims=True)
                         for j in range(DEC_GROUP)], axis=0)
    m = jnp.max(s, axis=-1, keepdims=True)
    if sink is not None:
        m = jnp.maximum(m, sink)
    p = jnp.exp(s - m)
    den = jnp.sum(p, axis=-1, keepdims=True)
    if sink is not None:
        den = den + jnp.exp(sink - m)
    p = p / den
    lane = lax.broadcasted_iota(jnp.int32, q.shape, 1)
    out = jnp.zeros(q.shape, F32)
    for j in range(DEC_GROUP):
        o_col = jnp.sum(vts[j] * p[j:j + 1, :], axis=-1, keepdims=True)
        out = jnp.where(lane == j, o_col, out)
    return out


def _mem_attn_decode_kernel(q_ref, k_ref, v_ref, o_ref):
    for hh in range(XATTN_HEADS):
        kts = [k_ref[0, j, hh] for j in range(DEC_GROUP)]
        vts = [v_ref[0, j, hh] for j in range(DEC_GROUP)]
        o_ref[0, hh] = _decode_heads(q_ref[0, hh] * ATTN_SCALE, kts, vts, None)


def mem_attn_decode(q_cols, cache_k, cache_v, layer):
    ng = q_cols.shape[0]
    qspec = pl.BlockSpec((1,) + q_cols.shape[1:], lambda i: (i, 0, 0, 0))
    cspec = pl.BlockSpec((1, DEC_GROUP) + cache_k.shape[2:], lambda i: (layer, i, 0, 0, 0))
    return pl.pallas_call(
        _mem_attn_decode_kernel,
        grid=(ng,),
        in_specs=[qspec, cspec, cspec],
        out_specs=qspec,
        out_shape=jax.ShapeDtypeStruct(q_cols.shape, F32),
        compiler_params=_params("parallel"),
        name="mem_attn_decode",
    )(q_cols, cache_k, cache_v)


def _swa_decode_kernel(*refs, shift):
    if shift:
        sink_ref, qp_ref, qr_ref, cos_ref, sin_ref, k_ref, v_ref, kn_ref, vn_ref, o_ref, ko_ref, vo_ref = refs
    else:
        sink_ref, qp_ref, qr_ref, cos_ref, sin_ref, k_ref, v_ref, o_ref = refs
    lane = lax.broadcasted_iota(jnp.int32, (HEAD_DIM, WINDOW), 1)
    cos = cos_ref[...]
    sin = sin_ref[...]
    for kvh in range(SWA_KV_HEADS):
        kts, vts = [], []
        for j in range(DEC_GROUP):
            kt = k_ref[j, kvh]
            vt = v_ref[j, kvh]
            if shift:
                kt = jnp.where(lane == WINDOW - 1, _col(kn_ref[0, kvh], j, WINDOW), pltpu.roll(kt, WINDOW - 1, 1))
                vt = jnp.where(lane == WINDOW - 1, _col(vn_ref[0, kvh], j, WINDOW), pltpu.roll(vt, WINDOW - 1, 1))
                ko_ref[j, kvh] = kt
                vo_ref[j, kvh] = vt
            kts.append(kt)
            vts.append(vt)
        for g in range(SWA_GROUP):
            hq = g * SWA_KV_HEADS + kvh
            q = (qp_ref[0, hq] * cos + qr_ref[0, hq] * sin) * ATTN_SCALE
            o_ref[0, hq] = _decode_heads(q, kts, vts, sink_ref[kvh * SWA_GROUP + g])


def swa_decode(sinks, qp_cols, qr_cols, cos_cols, sin_cols, k_win, v_win, k_new_cols=None, v_new_cols=None):
    shift = k_new_cols is not None
    ng = qp_cols.shape[0]
    qspec = pl.BlockSpec((1,) + qp_cols.shape[1:], lambda i: (i, 0, 0, 0))
    wspec = pl.BlockSpec((DEC_GROUP,) + k_win.shape[1:], lambda i: (i, 0, 0, 0))
    tab = _const_spec(cos_cols.shape)
    in_specs = [pl.BlockSpec(memory_space=pltpu.SMEM), qspec, qspec, tab, tab, wspec, wspec]
    args = [sinks, qp_cols, qr_cols, cos_cols, sin_cols, k_win, v_win]
    out_specs = [qspec]
    out_shape = [jax.ShapeDtypeStruct(qp_cols.shape, F32)]
    if shift:
        nspec = pl.BlockSpec((1,) + k_new_cols.shape[1:], lambda i: (i, 0, 0, 0))
        in_specs += [nspec, nspec]
        args += [k_new_cols, v_new_cols]
        out_specs += [wspec, wspec]
        out_shape += [jax.ShapeDtypeStruct(k_win.shape, F32)] * 2
    return pl.pallas_call(
        functools.partial(_swa_decode_kernel, shift=shift),
        grid=(ng,),
        in_specs=in_specs,
        out_specs=out_specs,
        out_shape=out_shape,
        compiler_params=_params("parallel"),
        name="swa_decode_shift" if shift else "swa_decode",
    )(*args)


def _rot_cols(w):
    d_in, n = w.shape
    w4 = w.reshape(d_in, n // HEAD_DIM, 2, HEAD_DIM // 2)
    return jnp.concatenate([-w4[:, :, 1:2], w4[:, :, 0:1]], axis=2).reshape(d_in, n)


def _rope_tables(positions, n_heads):
    half = HEAD_DIM // 2
    inv = ROPE_THETA ** (-jnp.arange(half, dtype=F32) / half)
    ang = positions.astype(F32)[:, None] * inv[None, :]
    cos = jnp.tile(jnp.cos(ang), (1, 2 * n_heads))
    sin = jnp.tile(jnp.sin(ang), (1, 2 * n_heads))
    return cos, sin


_SWA_PERM = np.array([(kvh * SWA_GROUP + g) * HEAD_DIM + d
                      for g in range(SWA_GROUP) for kvh in range(SWA_KV_HEADS) for d in range(HEAD_DIM)])

TM_ROWS = 512
TM_POST = 256
HGRN_TC = 512
HGRN_HB = 6
MEM_TL = 512
DEC_BB = 16


def _to_cols(x, n_heads):
    nb = x.shape[0]
    return x.reshape(nb // DEC_GROUP, DEC_GROUP, n_heads, x.shape[1] // n_heads).transpose(0, 2, 3, 1)


def _from_cols(x):
    ng, n_heads, d, _ = x.shape
    return x.transpose(0, 3, 1, 2).reshape(ng * DEC_GROUP, n_heads * d)


def kernel(x_prompt, x_sample, cache_mem_k, cache_mem_v, state_hgrn, state_swa_k, state_swa_v, mem_prompt,
           mem_norm, w_mem_kv, w_in_a, lb_logits, hgrn_norm, w_in_b, sinks, kv_norm, w_kv, w_out,
           norm_pre_mix, norm_post_mix, norm_pre_mlp, norm_post_mlp, w_up, w_down):
    bp, seq, d = x_prompt.shape
    bd = x_sample.shape[0]
    n_b = DEPTH - N_A_LAYERS

    w_in_a16 = w_in_a.astype(BF16)
    wq = w_in_b[:, :, :SWA_WIDTH]
    w_in_b16 = jnp.stack([
        jnp.concatenate([wq[j][:, _SWA_PERM], _rot_cols(wq[j])[:, _SWA_PERM], w_in_b[j][:, SWA_WIDTH:]], axis=1)
        for j in range(n_b)]).astype(BF16)
    wk = w_kv[:, :SWA_KV_WIDTH]
    w_kv16 = jnp.concatenate([wk, _rot_cols(wk), w_kv[:, SWA_KV_WIDTH:]], axis=1).astype(BF16)
    w_out16 = jnp.stack([w_out[l] if l < N_A_LAYERS else
                         jnp.concatenate([w_out[l][:SWA_WIDTH][_SWA_PERM], w_out[l][SWA_WIDTH:]], axis=0)
                         for l in range(DEPTH)]).astype(BF16)
    w_up16 = w_up.astype(BF16)
    w_down16 = w_down.astype(BF16)
    sinks_g = sinks.astype(F32)

    cos_p, sin_p = _rope_tables(jnp.arange(seq), SWA_KV_HEADS)
    cos_d, sin_d = _rope_tables(jnp.full((1,), PAST_LEN), SWA_KV_HEADS)

    mem_rows = mem_prompt.reshape(bp * N_MEM, d)
    mkv = mem_kv(mem_rows, mem_norm, w_mem_kv.astype(BF16))
    mem_k_prompt = mkv[:, :, :XATTN_WIDTH].reshape(DEPTH, bp, N_MEM, XATTN_HEADS, HEAD_DIM)
    mem_v_prompt = mkv[:, :, XATTN_WIDTH:].reshape(DEPTH, bp, N_MEM, XATTN_HEADS, HEAD_DIM)
    mk16 = mkv[:, :, :XATTN_WIDTH].reshape(DEPTH, bp, N_MEM, XATTN_WIDTH).astype(BF16)
    mv16 = mkv[:, :, XATTN_WIDTH:].reshape(DEPTH, bp, N_MEM, XATTN_WIDTH).astype(BF16)

    h = x_prompt.reshape(bp * seq, d)
    hgrn_prompt_states = []
    k_new = v_new = None
    for l in range(DEPTH):
        if l == N_A_LAYERS:
            k_new, v_new = kv_proj(h, kv_norm, w_kv16, cos_p, sin_p, TM_ROWS)
            k_new = k_new.reshape(bp, seq, SWA_KV_WIDTH)
            v_new = v_new.reshape(bp, seq, SWA_KV_WIDTH)
        if l < N_A_LAYERS:
            proj = norm_matmul(h, norm_pre_mix[l], w_in_a16[l], TM_ROWS)
            tok, s_fin = hgrn_prompt(proj, lb_logits, hgrn_norm[l], l, bp, seq, HGRN_TC, HGRN_HB)
            hgrn_prompt_states.append(s_fin)
            cq_block = 4 * HGRN_WIDTH // XATTN_WIDTH
        else:
            j = l - N_A_LAYERS
            proj = norm_matmul(h, norm_pre_mix[l], w_in_b16[j], TM_ROWS)
            tok = swa_prompt(proj, k_new, v_new, cos_p, sin_p, sinks_g[j], bp, seq)
            cq_block = 2 * SWA_WIDTH // XATTN_WIDTH
        xo = mem_attn_prompt(proj, cq_block, mk16[l], mv16[l], bp, seq, MEM_TL)
        h = post_mix(tok, xo, h, w_out16[l], norm_post_mix[l], norm_pre_mlp[l], w_up16[l], w_down16[l],
                     norm_post_mlp[l], TM_POST)
    y_prompt = h.reshape(bp, seq, d)
    hgrn_prompt_out = jnp.stack(hgrn_prompt_states)
    swa_k_prompt = k_new[:, -WINDOW:].reshape(bp, WINDOW, SWA_KV_HEADS, HEAD_DIM)
    swa_v_prompt = v_new[:, -WINDOW:].reshape(bp, WINDOW, SWA_KV_HEADS, HEAD_DIM)

    cache_k = cache_mem_k.transpose(0, 1, 3, 4, 2)
    cache_v = cache_mem_v.transpose(0, 1, 3, 4, 2)
    k_win = state_swa_k.transpose(0, 2, 3, 1)
    v_win = state_swa_v.transpose(0, 2, 3, 1)
    cos_cols = jnp.broadcast_to(cos_d[0, :HEAD_DIM, None], (HEAD_DIM, DEC_GROUP))
    sin_cols = jnp.broadcast_to(sin_d[0, :HEAD_DIM, None], (HEAD_DIM, DEC_GROUP))

    h = x_sample.reshape(bd, d)
    hgrn_sample = None
    for l in range(DEPTH):
        if l == N_A_LAYERS:
            k_tok, v_tok = kv_proj(h, kv_norm, w_kv16, jnp.tile(cos_d, (bd, 1)), jnp.tile(sin_d, (bd, 1)), bd)
        if l < N_A_LAYERS:
            proj = norm_matmul(h, norm_pre_mix[l], w_in_a16[l], bd)
            tok, hgrn_sample = hgrn_decode(
                proj, _to_cols(proj[:, :HGRN_WIDTH], HGRN_HEADS),
                _to_cols(proj[:, HGRN_WIDTH:2 * HGRN_WIDTH], HGRN_HEADS), lb_logits, hgrn_norm[l], state_hgrn, l,
                DEC_BB, new_states=hgrn_sample)
            cq = proj[:, 4 * HGRN_WIDTH:]
        else:
            j = l - N_A_LAYERS
            proj = norm_matmul(h, norm_pre_mix[l], w_in_b16[j], bd)
            n_q = SWA_GROUP * SWA_KV_HEADS
            qp_cols = _to_cols(proj[:, :SWA_WIDTH], n_q)
            qr_cols = _to_cols(proj[:, SWA_WIDTH:2 * SWA_WIDTH], n_q)
            if l == N_A_LAYERS:
                o_cols, k_win, v_win = swa_decode(sinks_g[j], qp_cols, qr_cols, cos_cols, sin_cols, k_win, v_win,
                                                  _to_cols(k_tok, SWA_KV_HEADS), _to_cols(v_tok, SWA_KV_HEADS))
            else:
                o_cols = swa_decode(sinks_g[j], qp_cols, qr_cols, cos_cols, sin_cols, k_win, v_win)[0]
            tok = _from_cols(o_cols)
            cq = proj[:, 2 * SWA_WIDTH:]
        xo = _from_cols(mem_attn_decode(_to_cols(cq, XATTN_HEADS), cache_k, cache_v, l))
        h = post_mix(tok, xo, h, w_out16[l], norm_post_mix[l], norm_pre_mlp[l], w_up16[l], w_down16[l],
                     norm_post_mlp[l], bd)
    y_sample = h.reshape(bd, 1, d)
    swa_k_sample = k_win.transpose(0, 3, 1, 2)
    swa_v_sample = v_win.transpose(0, 3, 1, 2)

    return (y_prompt, y_sample, mem_k_prompt, mem_v_prompt, hgrn_prompt_out, swa_k_prompt, swa_v_prompt,
            hgrn_sample, swa_k_sample, swa_v_sample)
```

```python
import functools

import jax
import jax.numpy as jnp
import numpy as np
from jax import lax
from jax.experimental import pallas as pl
from jax.experimental.pallas import tpu as pltpu

F32 = jnp.float32
BF16 = jnp.bfloat16

D_MODEL = 1024
DEPTH = 4
N_A_LAYERS = 2
N_MEM = 256
XATTN_HEADS = 4
XATTN_WIDTH = 256
HEAD_DIM = 64
HGRN_HEADS = 6
HGRN_DIM = 128
HGRN_WIDTH = HGRN_HEADS * HGRN_DIM
SWA_KV_HEADS = 4
SWA_GROUP = 3
SWA_WIDTH = 768
SWA_KV_WIDTH = 256
WINDOW = 128
PAST_LEN = 8192
ROPE_THETA = 10000.0
D_FF = 4096
EPS = 1e-6
ATTN_SCALE = HEAD_DIM ** -0.5

V7X_VMEM_BYTES = 64 * 1024 * 1024
VMEM_LIMIT_BYTES = V7X_VMEM_BYTES - 8 * 1024 * 1024
NEG_BIG = -1e30

NT_DIMS = (((1,), (1,)), ((), ()))


def _params(*semantics):
    return pltpu.CompilerParams(dimension_semantics=semantics, vmem_limit_bytes=VMEM_LIMIT_BYTES)


def _rms(x, gain):
    return x * lax.rsqrt(jnp.mean(x * x, axis=-1, keepdims=True) + EPS) * gain


def _dot(a, b):
    return jnp.dot(a, b, preferred_element_type=F32)


def _dot_nt(a, b):
    return lax.dot_general(a, b, NT_DIMS, preferred_element_type=F32)


def _sigmoid_pair(z):
    e = jnp.exp(-jnp.abs(z))
    r = 1.0 / (1.0 + e)
    er = e * r
    pos = z >= 0
    return jnp.where(pos, r, er), jnp.where(pos, er, r)


def _silu(x):
    return x / (1.0 + jnp.exp(-x))


ROPE_TABLE_WIDTH = 2 * HEAD_DIM


def _widen(table, width):
    return jnp.concatenate([table] * (width // ROPE_TABLE_WIDTH), axis=1)


def _const_spec(shape):
    nd = len(shape)
    return pl.BlockSpec(shape, lambda *_: (0,) * nd)


def _norm_matmul_kernel(x_ref, g_ref, w_ref, o_ref):
    y = _rms(x_ref[...], g_ref[...])
    o_ref[...] = _dot(y.astype(BF16), w_ref[0])


def _layer_weight(w, layer):
    return pl.BlockSpec((1,) + w.shape[1:], lambda *_: (layer, 0, 0), pipeline_mode=pl.Buffered(1))


def norm_matmul(x, gain, w, layer, tm):
    m, d = x.shape
    n = w.shape[2]
    return pl.pallas_call(
        _norm_matmul_kernel,
        grid=(m // tm,),
        in_specs=[pl.BlockSpec((tm, d), lambda i: (i, 0)), _const_spec((1, d)), _layer_weight(w, layer)],
        out_specs=pl.BlockSpec((tm, n), lambda i: (i, 0)),
        out_shape=jax.ShapeDtypeStruct((m, n), F32),
        compiler_params=_params("parallel"),
        name="norm_matmul",
    )(x, gain.reshape(1, d), w)


def _mem_kv_kernel(x_ref, g_ref, w_ref, o_ref):
    y = _rms(x_ref[...], g_ref[0])
    o_ref[0] = _dot(y.astype(BF16), w_ref[0])


def mem_kv(mem, gains, w):
    m, d = mem.shape
    depth, _, n = w.shape
    return pl.pallas_call(
        _mem_kv_kernel,
        grid=(depth,),
        in_specs=[_const_spec((m, d)), pl.BlockSpec((1, 1, d), lambda l: (l, 0, 0)),
                  pl.BlockSpec((1, d, n), lambda l: (l, 0, 0))],
        out_specs=pl.BlockSpec((1, m, n), lambda l: (l, 0, 0)),
        out_shape=jax.ShapeDtypeStruct((depth, m, n), F32),
        compiler_params=_params("parallel"),
        name="mem_kv",
    )(mem, gains.reshape(depth, 1, d), w)


def _norm_matmul_kv_kernel(x_ref, g_ref, w_ref, gkv_ref, wkv_ref, cos_ref, sin_ref, o_ref, k_ref, v_ref):
    x = x_ref[...]
    xn = x * lax.rsqrt(jnp.mean(x * x, axis=-1, keepdims=True) + EPS)
    o_ref[...] = _dot((xn * g_ref[...]).astype(BF16), w_ref[0])
    acc = _dot((xn * gkv_ref[...]).astype(BF16), wkv_ref[...])
    kw = SWA_KV_WIDTH
    k_ref[...] = acc[:, :kw] * _widen(cos_ref[...], kw) + acc[:, kw:2 * kw] * _widen(sin_ref[...], kw)
    v_ref[...] = acc[:, 2 * kw:]


def norm_matmul_kv(x, gain, w, layer, gain_kv, w_kv, cos, sin, tm):
    m, d = x.shape
    n = w.shape[2]
    kw = SWA_KV_WIDTH
    n_pos_tiles = cos.shape[0] // tm
    tab = pl.BlockSpec((tm, ROPE_TABLE_WIDTH), lambda i: (i % n_pos_tiles, 0))
    row = lambda width: pl.BlockSpec((tm, width), lambda i: (i, 0))
    return pl.pallas_call(
        _norm_matmul_kv_kernel,
        grid=(m // tm,),
        in_specs=[row(d), _const_spec((1, d)), _layer_weight(w, layer), _const_spec((1, d)),
                  _const_spec(w_kv.shape), tab, tab],
        out_specs=[row(n), row(kw), row(kw)],
        out_shape=[jax.ShapeDtypeStruct((m, n), F32)] + [jax.ShapeDtypeStruct((m, kw), F32)] * 2,
        compiler_params=_params("parallel"),
        name="norm_matmul_kv",
    )(x, gain.reshape(1, d), w, gain_kv.reshape(1, d), w_kv, cos, sin)


def _post_mix_compute(mix_in, h, wo_ref, g1, g2, wup_ref, wdn_ref, g3):
    h1 = h + _rms(_dot(mix_in.astype(BF16), wo_ref[0]), g1)
    u = jnp.maximum(_dot(_rms(h1, g2).astype(BF16), wup_ref[0]), 0.0)
    return h1 + _rms(_dot((u * u).astype(BF16), wdn_ref[0]), g3)


MLP_SLICE = 256


class _SlicedPostMix:
    def __init__(self, mix_in, h, wo_ref, g1, g2, wup_ref, wdn_ref, g3, sink):
        d = h.shape[1]
        self.n_slices = (2 * d + D_FF) // MLP_SLICE
        self.done = 0
        self._steps = self._emit(mix_in, h, wo_ref, g1, g2, wup_ref, wdn_ref, g3, sink)

    @staticmethod
    def _emit(mix_in, h, wo_ref, g1, g2, wup_ref, wdn_ref, g3, sink):
        d = h.shape[1]
        x16 = mix_in.astype(BF16)
        parts = []
        for n in range(0, d, MLP_SLICE):
            parts.append(_dot(x16, wo_ref[0, :, n:n + MLP_SLICE]))
            yield
        h1 = h + _rms(jnp.concatenate(parts, axis=1), g1)
        hn = _rms(h1, g2).astype(BF16)
        parts = []
        for n in range(0, D_FF, MLP_SLICE):
            u = jnp.maximum(_dot(hn, wup_ref[0, :, n:n + MLP_SLICE]), 0.0)
            parts.append((u * u).astype(BF16))
            yield
        u16 = jnp.concatenate(parts, axis=1)
        parts = []
        for n in range(0, d, MLP_SLICE):
            parts.append(_dot(u16, wdn_ref[0, :, n:n + MLP_SLICE]))
            if n + MLP_SLICE == d:
                sink(h1 + _rms(jnp.concatenate(parts, axis=1), g3))
            yield

    def tick(self, frac):
        while self.done < self.n_slices and self.done < frac * self.n_slices:
            next(self._steps)
            self.done += 1


def _post_mix_kernel(tok_ref, xo_ref, h_ref, wo_ref, g1_ref, g2_ref, wup_ref, wdn_ref, g3_ref, o_ref):
    mix_in = jnp.concatenate([tok_ref[...], xo_ref[...]], axis=1)
    o_ref[...] = _post_mix_compute(mix_in, h_ref[...], wo_ref, g1_ref[...], g2_ref[...], wup_ref, wdn_ref,
                                   g3_ref[...])


def post_mix(tok, xo, h, layer, wo, g_post_mix, g_pre_mlp, wup, wdn, g_post_mlp, tm):
    m, d = h.shape
    row = lambda w: pl.BlockSpec((tm, w), lambda i: (i, 0))
    return pl.pallas_call(
        _post_mix_kernel,
        grid=(m // tm,),
        in_specs=[row(tok.shape[1]), row(xo.shape[1]), row(d), _layer_weight(wo, layer), _const_spec((1, d)),
                  _const_spec((1, d)), _layer_weight(wup, layer), _layer_weight(wdn, layer), _const_spec((1, d))],
        out_specs=row(d),
        out_shape=jax.ShapeDtypeStruct((m, d), F32),
        compiler_params=_params("parallel"),
        name="post_mix",
    )(tok, xo, h, wo, g_post_mix.reshape(1, d), g_pre_mlp.reshape(1, d), wup, wdn, g_post_mlp.reshape(1, d))


HGRN_CHUNK = 128
MLP_ROWS = 256
HGRN_LEVELS = (64, 32, 16, 8, 4, 2, 1)


def _layer_lower_bound(logits, layer):
    m = jnp.max(logits, axis=0, keepdims=True)
    e = jnp.exp(logits - m)
    p = e / jnp.sum(e, axis=0, keepdims=True)
    lb = jnp.zeros(p.shape[1:], F32)
    for j in range(1, layer + 1):
        lb = lb + p[j]
    return lb


def _split3(x):
    a = x.astype(BF16)
    r = x - a.astype(F32)
    b = r.astype(BF16)
    c = (r - b.astype(F32)).astype(BF16)
    return a, b, c


def _level_operands(q, k, b, w):
    n, lanes = b.shape
    if w >= 8:
        zero = jnp.zeros((w, lanes), F32)
        a_parts, b_parts = [], []
        for s in range(0, n, 2 * w):
            ref = b[s + w - 1:s + w, :]
            lo, hi = slice(s, s + w), slice(s + w, s + 2 * w)
            b_parts += [k[lo] * jnp.exp(ref - b[lo]), zero]
            a_parts += [zero, q[hi] * jnp.exp(b[hi] - ref)]
        return jnp.concatenate(a_parts, axis=0), jnp.concatenate(b_parts, axis=0)
    sub = lax.broadcasted_iota(jnp.int32, (8, lanes), 0)
    pieces = []
    for s in range(0, n, 8):
        blk = None
        for t in range(0, 8, 2 * w):
            cand = jnp.broadcast_to(b[s + t + w - 1:s + t + w, :], (8, lanes))
            blk = cand if blk is None else jnp.where(sub >= t, cand, blk)
        pieces.append(blk)
    d = b - jnp.concatenate(pieces, axis=0)
    upper = (lax.broadcasted_iota(jnp.int32, (n, lanes), 0) & w) != 0
    e = jnp.exp(jnp.where(upper, d, -d))
    return jnp.where(upper, q * e, 0.0), jnp.where(upper, 0.0, k * e)


HGRN_TICK_ACT, HGRN_TICK_CUMSUM, HGRN_TICK_LEVEL, HGRN_TICK_DOTS, HGRN_TICK_DIAG, HGRN_TICK_OUT = 30, 8, 8, 2, 3, 7
HGRN_TICK_TOTAL = (HGRN_TICK_ACT + HGRN_TICK_CUMSUM + len(HGRN_LEVELS) * (HGRN_TICK_LEVEL + HGRN_TICK_DOTS)
                   + HGRN_TICK_DIAG + (HGRN_HEADS // 2) * HGRN_TICK_OUT)


def _hgrn_chunk(qraw, z, v, gate, lb, gain, st_ref, tick=lambda weight: None):
    c, dh = HGRN_CHUNK, HGRN_DIM
    heads = [slice(hh * dh, (hh + 1) * dh) for hh in range(qraw.shape[1] // dh)]
    tick(HGRN_TICK_ACT)
    sig_p, sig_n = _sigmoid_pair(z)
    logf = jnp.log(lb + (1.0 - lb) * sig_p)
    k = (1.0 - lb) * sig_n
    q = _silu(qraw)

    ti = lax.broadcasted_iota(jnp.int32, (c, c), 0)
    si = lax.broadcasted_iota(jnp.int32, (c, c), 1)
    tril = (si <= ti).astype(BF16)
    tick(HGRN_TICK_CUMSUM)
    b = sum(_dot(tril, part) for part in _split3(logf))

    txs = ti ^ si
    scores = [None] * len(heads)
    for w in HGRN_LEVELS:
        tick(HGRN_TICK_LEVEL)
        a_w, b_w = _level_operands(q, k, b, w)
        a_w, b_w = a_w.astype(BF16), b_w.astype(BF16)
        tick(HGRN_TICK_DOTS)
        for hh, ln in enumerate(heads):
            s_w = _dot_nt(a_w[:, ln], b_w[:, ln])
            scores[hh] = s_w if scores[hh] is None else jnp.where(txs < 2 * w, s_w, scores[hh])
    tick(HGRN_TICK_DIAG)
    q16, k16 = q.astype(BF16), k.astype(BF16)
    scores = [jnp.where(txs == 0, _dot_nt(q16[:, ln], k16[:, ln]), scores[hh]).astype(BF16)
              for hh, ln in enumerate(heads)]

    b_last = b[c - 1:c, :]
    qe = (q * jnp.exp(b)).astype(BF16)
    ke = (k * jnp.exp(b_last - b)).astype(BF16)
    decay = jnp.exp(b_last)
    v16 = v.astype(BF16)
    outs = []
    for hh, ln in enumerate(heads):
        if hh % 2 == 0:
            tick(HGRN_TICK_OUT)
        st = st_ref[hh]
        o = _dot_nt(qe[:, ln], st.astype(BF16)) + _dot(scores[hh], v16[:, ln])
        st_ref[hh] = st * decay[:, ln] + _dot(v[:, ln].T.astype(BF16), ke[:, ln])
        outs.append(_rms(o, gain[:, ln]))
    return jnp.concatenate(outs, axis=1) * _silu(gate)


DEC_GROUP = 8


def _hgrn_decode_kernel(*refs, layer, n_groups):
    qc_ref, zc_ref, i_ref, g_ref, lblc_ref, gain_ref, s_ref = refs[:7]
    o_ref, snew_ref = refs[-2:]
    lb = _layer_lower_bound(lblc_ref[:, 0], layer)
    gain = gain_ref[0]
    dh = HGRN_DIM

    def body(gi, carry):
        sig_p, sig_n = _sigmoid_pair(zc_ref[gi, 0])
        f = lb + (1.0 - lb) * sig_p
        k = (1.0 - lb) * sig_n
        q = _silu(qc_ref[gi, 0])
        rows = []
        for j in range(DEC_GROUP):
            r = gi * DEC_GROUP + j
            v_row = i_ref[pl.ds(r, 1), :]
            s_new = (jnp.broadcast_to(f[:, j:j + 1], (dh, dh)) * s_ref[0, r, 0]
                     + jnp.broadcast_to(k[:, j:j + 1], (dh, dh)) * v_row)
            snew_ref[0, r, 0] = s_new
            rows.append(jnp.sum(jnp.broadcast_to(q[:, j:j + 1], (dh, dh)) * s_new, axis=0, keepdims=True))
        o = jnp.concatenate(rows, axis=0)
        grp = pl.ds(pl.multiple_of(gi * DEC_GROUP, DEC_GROUP), DEC_GROUP)
        o_ref[grp, :] = _rms(o, gain) * _silu(g_ref[grp, :])
        return carry

    lax.fori_loop(0, n_groups, body, 0)


def hgrn_decode(proj, q_cols, z_cols, lb_logits, gain, states, layer, bb, new_states=None):
    n_layers, nb = states.shape[:2]
    h, dh = HGRN_HEADS, HGRN_DIM
    n_groups = bb // DEC_GROUP
    col = pl.BlockSpec((n_groups, 1, dh, DEC_GROUP), lambda b, hh: (b, hh, 0, 0))
    sec = lambda s: pl.BlockSpec((bb, dh), lambda b, hh: (b, s * h + hh))
    st = pl.BlockSpec((1, bb, 1, dh, dh), lambda b, hh: (layer, b, hh, 0, 0))
    in_specs = [col, col, sec(2), sec(3),
                pl.BlockSpec((n_layers, 1, dh, 1), lambda b, hh: (0, hh, 0, 0)),
                pl.BlockSpec((1, 1, dh), lambda b, hh: (hh, 0, 0)), st]
    args = [q_cols, z_cols, proj, proj, lb_logits.reshape(n_layers, h, dh, 1), gain.reshape(h, 1, dh), states]
    aliases = {}
    if new_states is not None:
        in_specs.append(pl.BlockSpec(memory_space=pl.ANY))
        args.append(new_states)
        aliases = {len(args) - 1: 1}
    kern = functools.partial(_hgrn_decode_kernel, layer=layer, n_groups=n_groups)
    return pl.pallas_call(
        kern,
        grid=(nb // bb, h),
        in_specs=in_specs,
        out_specs=[pl.BlockSpec((bb, dh), lambda b, hh: (b, hh)), st],
        out_shape=[jax.ShapeDtypeStruct((nb, h * dh), F32), jax.ShapeDtypeStruct(states.shape, F32)],
        input_output_aliases=aliases,
        compiler_params=_params("parallel", "parallel"),
        name="hgrn_decode",
    )(*args)


def _lane_head(shape):
    return lax.broadcasted_iota(jnp.int32, shape, 1) // HEAD_DIM


def _mem_attn_compute(q, mk, mv):
    q = q * ATTN_SCALE
    tl = q.shape[0]
    head = _lane_head(q.shape)
    qs = jnp.concatenate([jnp.where(head == hh, q, 0.0) for hh in range(XATTN_HEADS)], axis=0).astype(BF16)
    s = _dot_nt(qs, mk)
    p = jnp.exp(s - jnp.max(s, axis=-1, keepdims=True))
    p = (p / jnp.sum(p, axis=-1, keepdims=True)).astype(BF16)
    o = _dot(p, mv)
    out = jnp.zeros(q.shape, F32)
    for hh in range(XATTN_HEADS):
        out = jnp.where(head == hh, o[hh * tl:(hh + 1) * tl, :], out)
    return out


SWA_TICKS = SWA_GROUP * SWA_KV_HEADS


def _swa_block_compute(q, keys, vals, nblk, sink_ref, tick=lambda weight: None):
    qb = WINDOW
    kw = SWA_KV_WIDTH
    head = _lane_head((qb, kw))
    combos = [(g, kvh) for g in range(SWA_GROUP) for kvh in range(SWA_KV_HEADS)]
    qs = jnp.concatenate([jnp.where(head == kvh, q[:, g * kw:(g + 1) * kw], 0.0) for g, kvh in combos],
                         axis=0).astype(BF16)
    s = _dot_nt(qs, keys)

    i_q = lax.broadcasted_iota(jnp.int32, (qb, 2 * qb), 0)
    j_k = lax.broadcasted_iota(jnp.int32, (qb, 2 * qb), 1)
    rel = qb + i_q - j_k
    valid = (rel >= 0) & (rel < WINDOW) & (j_k + nblk * qb >= qb)
    outs = []
    for g in range(SWA_GROUP):
        acc = jnp.zeros((qb, kw), F32)
        for kvh in range(SWA_KV_HEADS):
            idx = g * SWA_KV_HEADS + kvh
            sink = sink_ref[kvh * SWA_GROUP + g]
            sc = jnp.where(valid, s[idx * qb:(idx + 1) * qb, :], NEG_BIG)
            m = jnp.maximum(jnp.max(sc, axis=-1, keepdims=True), sink)
            p = jnp.exp(sc - m)
            den = jnp.sum(p, axis=-1, keepdims=True) + jnp.exp(sink - m)
            o = _dot((p / den).astype(BF16), vals)
            acc = jnp.where(head == kvh, o, acc)
            tick(1)
        outs.append(acc)
    return jnp.concatenate(outs, axis=1)


def _hgrn_layer_kernel(p_ref, lbl_ref, gain_ref, mk_ref, mv_ref, h_ref, wo_ref,
                       g1_ref, g2_ref, wup_ref, wdn_ref, g3_ref, o_ref, sfin_ref, st_ref, mix_ref, *, layer,
                       n_chunks, tiles_per_seq):
    step = pl.program_id(0)
    tw = HGRN_WIDTH
    tile_in_seq = lax.rem(step, tiles_per_seq)

    @pl.when(step == 0)
    def _():
        mix_ref[...] = jnp.zeros(mix_ref.shape, F32)

    @pl.when(tile_in_seq == 0)
    def _():
        st_ref[...] = jnp.zeros(st_ref.shape, F32)

    slot = lax.rem(step, 2)
    lb = _layer_lower_bound(lbl_ref[...], layer)
    gain = gain_ref[...]
    sub = MLP_ROWS // HGRN_CHUNK

    def part(pi, carry):
        mrows = pl.ds(pl.multiple_of(pi * MLP_ROWS, MLP_ROWS), MLP_ROWS)

        def sink(res):
            o_ref[mrows, :] = res

        mlp = _SlicedPostMix(mix_ref[1 - slot, mrows, :], h_ref[mrows, :], wo_ref, g1_ref[...], g2_ref[...],
                             wup_ref, wdn_ref, g3_ref[...], sink)
        ticks = [0]

        def tick(weight):
            ticks[0] += weight
            mlp.tick(ticks[0] / (sub * HGRN_TICK_TOTAL))

        for cj in range(sub):
            rows = pl.ds(pl.multiple_of((pi * sub + cj) * HGRN_CHUNK, HGRN_CHUNK), HGRN_CHUNK)
            q, f, i, g = (p_ref[rows, k * tw:(k + 1) * tw] for k in range(4))
            mix_ref[slot, rows, :tw] = _hgrn_chunk(q, f, i, g, lb, gain, st_ref, tick)
        mlp.tick(1.0)
        mix_ref[slot, mrows, tw:] = _mem_attn_compute(p_ref[mrows, 4 * tw:], mk_ref[0], mv_ref[0])
        return carry

    lax.fori_loop(0, n_chunks // sub, part, 0)

    @pl.when(tile_in_seq == tiles_per_seq - 1)
    def _():
        for hh in range(HGRN_HEADS):
            sfin_ref[0, hh] = st_ref[hh].T


def _layer_specs(batch, seq, tc, d):
    nct = seq // tc
    n_tiles = batch * nct
    cur = lambda s: jnp.minimum(s, n_tiles - 1)
    vec = pl.BlockSpec((1, d), lambda s: (0, 0))
    hrow = pl.BlockSpec((tc, d), lambda s: (jnp.maximum(s - 1, 0), 0))
    mem = pl.BlockSpec((1, N_MEM, XATTN_WIDTH), lambda s: (cur(s) // nct, 0, 0))
    return nct, n_tiles, cur, vec, hrow, mem


def hgrn_layer_prompt(proj, h, lb_logits, gain, layer, mk, mv, post_layer, wo, g_post_mix, g_pre_mlp, wup, wdn,
                      g_post_mlp, batch, seq, tc):
    d = h.shape[1]
    nh, dh = HGRN_HEADS, HGRN_DIM
    nct, n_tiles, cur, vec, hrow, mem = _layer_specs(batch, seq, tc, d)
    n_layers = lb_logits.shape[0]
    kern = functools.partial(_hgrn_layer_kernel, layer=layer, n_chunks=tc // HGRN_CHUNK, tiles_per_seq=nct)
    return pl.pallas_call(
        kern,
        grid=(n_tiles + 1,),
        in_specs=[pl.BlockSpec((tc, proj.shape[1]), lambda s: (cur(s), 0)),
                  pl.BlockSpec((n_layers, 1, nh * dh), lambda s: (0, 0, 0)),
                  pl.BlockSpec((1, nh * dh), lambda s: (0, 0)), mem, mem, hrow,
                  _layer_weight(wo, post_layer), vec, vec, _layer_weight(wup, post_layer),
                  _layer_weight(wdn, post_layer), vec],
        out_specs=[hrow, pl.BlockSpec((1, nh, dh, dh), lambda s: (cur(s) // nct, 0, 0, 0))],
        out_shape=[jax.ShapeDtypeStruct(h.shape, F32), jax.ShapeDtypeStruct((batch, nh, dh, dh), F32)],
        scratch_shapes=[pltpu.VMEM((nh, dh, dh), F32), pltpu.VMEM((2, tc, d), F32)],
        compiler_params=_params("arbitrary"),
        name="hgrn_layer_prompt",
    )(proj, lb_logits.reshape(n_layers, 1, nh * dh), gain.reshape(1, nh * dh), mk, mv, h,
      wo, g_post_mix.reshape(1, d), g_pre_mlp.reshape(1, d), wup, wdn, g_post_mlp.reshape(1, d))


def _swa_layer_kernel(sink_ref, p_ref, cos_ref, sin_ref, kp_ref, kc_ref, vp_ref, vc_ref, mk_ref,
                      mv_ref, h_ref, wo_ref, g1_ref, g2_ref, wup_ref, wdn_ref, g3_ref, o_ref, mix_ref, *, n_blocks,
                      tiles_per_seq):
    step = pl.program_id(0)
    n_tiles = pl.num_programs(0) - 1
    tw = SWA_WIDTH
    qb = WINDOW

    @pl.when(step == 0)
    def _():
        mix_ref[...] = jnp.zeros(mix_ref.shape, F32)

    slot = lax.rem(step, 2)
    tile_in_seq = lax.rem(jnp.minimum(step, n_tiles - 1), tiles_per_seq)
    q = (p_ref[:, :tw] * _widen(cos_ref[...], tw) + p_ref[:, tw:2 * tw] * _widen(sin_ref[...], tw)) * ATTN_SCALE
    k_rows = jnp.concatenate([kp_ref[0], kc_ref[0]], axis=0).astype(BF16)
    v_rows = jnp.concatenate([vp_ref[0], vc_ref[0]], axis=0).astype(BF16)
    sub = MLP_ROWS // qb
    for pi in range(n_blocks // sub):
        mrows = slice(pi * MLP_ROWS, (pi + 1) * MLP_ROWS)

        def sink(res, mrows=mrows):
            o_ref[mrows, :] = res

        mlp = _SlicedPostMix(mix_ref[1 - slot, mrows, :], h_ref[mrows, :], wo_ref, g1_ref[...], g2_ref[...],
                             wup_ref, wdn_ref, g3_ref[...], sink)
        ticks = [0]

        def tick(weight, mlp=mlp, ticks=ticks):
            ticks[0] += weight
            mlp.tick(ticks[0] / (sub * SWA_TICKS))

        for blk in range(pi * sub, (pi + 1) * sub):
            rows = slice(blk * qb, (blk + 1) * qb)
            span = slice(blk * qb, (blk + 2) * qb)
            mix_ref[slot, rows, :tw] = _swa_block_compute(q[rows], k_rows[span], v_rows[span],
                                                         tile_in_seq * n_blocks + blk, sink_ref, tick)
        mlp.tick(1.0)
        mix_ref[slot, mrows, tw:] = _mem_attn_compute(p_ref[mrows, 2 * tw:], mk_ref[0], mv_ref[0])


def swa_layer_prompt(proj, h, k_new, v_new, cos, sin, sinks, mk, mv, post_layer, wo, g_post_mix, g_pre_mlp, wup,
                     wdn, g_post_mlp, batch, seq, tc):
    d = h.shape[1]
    qb = WINDOW
    kw = SWA_KV_WIDTH
    nct, n_tiles, cur, vec, hrow, mem = _layer_specs(batch, seq, tc, d)
    seq_of = lambda s: cur(s) // nct
    tile = lambda s: lax.rem(cur(s), nct)
    tab = pl.BlockSpec((tc, ROPE_TABLE_WIDTH), lambda s: (tile(s), 0))
    prev = pl.BlockSpec((1, qb, kw), lambda s: (seq_of(s), jnp.maximum(tile(s) * (tc // qb) - 1, 0), 0))
    curkv = pl.BlockSpec((1, tc, kw), lambda s: (seq_of(s), tile(s), 0))
    kern = functools.partial(_swa_layer_kernel, n_blocks=tc // qb, tiles_per_seq=nct)
    return pl.pallas_call(
        kern,
        grid=(n_tiles + 1,),
        in_specs=[pl.BlockSpec(memory_space=pltpu.SMEM),
                  pl.BlockSpec((tc, proj.shape[1]), lambda s: (cur(s), 0)),
                  tab, tab, prev, curkv, prev, curkv, mem, mem, hrow,
                  _layer_weight(wo, post_layer), vec, vec, _layer_weight(wup, post_layer),
                  _layer_weight(wdn, post_layer), vec],
        out_specs=hrow,
        out_shape=jax.ShapeDtypeStruct(h.shape, F32),
        scratch_shapes=[pltpu.VMEM((2, tc, d), F32)],
        compiler_params=_params("arbitrary"),
        name="swa_layer_prompt",
    )(sinks, proj, cos, sin, k_new, k_new, v_new, v_new, mk, mv, h,
      wo, g_post_mix.reshape(1, d), g_pre_mlp.reshape(1, d), wup, wdn, g_post_mlp.reshape(1, d))


Q_ROWS = 8


def _col(x, j, width):
    return jnp.broadcast_to(x[:, j:j + 1], (x.shape[0], width))


def _decode_attend(units):
    s = jnp.concatenate([_dot(q.astype(BF16), kt.astype(BF16)) for q, kt, _, _ in units], axis=0)
    m = jnp.max(s, axis=-1, keepdims=True)
    use_sink = units[0][3] is not None
    if use_sink:
        sink = jnp.concatenate([u[3] for u in units], axis=0)
        m = jnp.maximum(m, sink)
    p = jnp.exp(s - m)
    den = jnp.sum(p, axis=-1, keepdims=True)
    if use_sink:
        den = den + jnp.exp(sink - m)
    p = (p / den).astype(BF16)
    return [_dot_nt(p[i * Q_ROWS:(i + 1) * Q_ROWS], u[2].astype(BF16)) for i, u in enumerate(units)]


def _mem_attn_decode_kernel(q_ref, k_ref, v_ref, o_ref, *, n_groups):
    def group(gi, carry):
        pairs = [(gi * DEC_GROUP + j, hh) for j in range(DEC_GROUP) for hh in range(XATTN_HEADS)]
        outs = _decode_attend([(q_ref[r, hh] * ATTN_SCALE, k_ref[0, r, hh], v_ref[0, r, hh], None)
                               for r, hh in pairs])
        for (r, hh), out in zip(pairs, outs):
            o_ref[r, hh] = out
        return carry

    lax.fori_loop(0, n_groups, group, 0)


def mem_attn_decode(q_rows, cache_k, cache_v, layer, bb):
    nb = q_rows.shape[0]
    qspec = pl.BlockSpec((bb,) + q_rows.shape[1:], lambda i: (i, 0, 0, 0))
    cspec = pl.BlockSpec((1, bb) + cache_k.shape[2:], lambda i: (layer, i, 0, 0, 0))
    return pl.pallas_call(
        functools.partial(_mem_attn_decode_kernel, n_groups=bb // DEC_GROUP),
        grid=(nb // bb,),
        in_specs=[qspec, cspec, cspec],
        out_specs=qspec,
        out_shape=jax.ShapeDtypeStruct(q_rows.shape, F32),
        compiler_params=_params("parallel"),
        name="mem_attn_decode",
    )(q_rows, cache_k, cache_v)


def _swa_decode_kernel(*refs, shift, n_groups):
    if shift:
        sink_ref, qp_ref, qr_ref, cos_ref, sin_ref, k_ref, v_ref, kn_ref, vn_ref, o_ref, ko_ref, vo_ref = refs
    else:
        sink_ref, qp_ref, qr_ref, cos_ref, sin_ref, k_ref, v_ref, o_ref = refs
    lane = lax.broadcasted_iota(jnp.int32, (HEAD_DIM, WINDOW), 1)
    row = lax.broadcasted_iota(jnp.int32, (Q_ROWS, 1), 0)
    cos = cos_ref[...]
    sin = sin_ref[...]
    sinks = []
    for kvh in range(SWA_KV_HEADS):
        col = jnp.zeros((Q_ROWS, 1), F32)
        for g in range(SWA_GROUP):
            col = jnp.where(row == g, sink_ref[kvh * SWA_GROUP + g], col)
        sinks.append(col)

    def group(gi, carry):
        pairs, units = [], []
        for j in range(DEC_GROUP):
            r = gi * DEC_GROUP + j
            for kvh in range(SWA_KV_HEADS):
                kt = k_ref[r, kvh]
                vt = v_ref[r, kvh]
                if shift:
                    kt = jnp.where(lane == WINDOW - 1, _col(kn_ref[gi, kvh], j, WINDOW),
                                   pltpu.roll(kt, WINDOW - 1, 1))
                    vt = jnp.where(lane == WINDOW - 1, _col(vn_ref[gi, kvh], j, WINDOW),
                                   pltpu.roll(vt, WINDOW - 1, 1))
                    ko_ref[r, kvh] = kt
                    vo_ref[r, kvh] = vt
                q = (qp_ref[r, kvh] * cos + qr_ref[r, kvh] * sin) * ATTN_SCALE
                pairs.append((r, kvh))
                units.append((q, kt, vt, sinks[kvh]))
        for (r, kvh), out in zip(pairs, _decode_attend(units)):
            o_ref[r, kvh] = out
        return carry

    lax.fori_loop(0, n_groups, group, 0)


def swa_decode(sinks, qp_rows, qr_rows, cos_row, sin_row, k_win, v_win, bb, k_new_cols=None, v_new_cols=None):
    shift = k_new_cols is not None
    nb = qp_rows.shape[0]
    qspec = pl.BlockSpec((bb,) + qp_rows.shape[1:], lambda i: (i, 0, 0, 0))
    wspec = pl.BlockSpec((bb,) + k_win.shape[1:], lambda i: (i, 0, 0, 0))
    tab = _const_spec(cos_row.shape)
    in_specs = [pl.BlockSpec(memory_space=pltpu.SMEM), qspec, qspec, tab, tab, wspec, wspec]
    args = [sinks, qp_rows, qr_rows, cos_row, sin_row, k_win, v_win]
    out_specs = [qspec]
    out_shape = [jax.ShapeDtypeStruct(qp_rows.shape, F32)]
    if shift:
        nspec = pl.BlockSpec((bb // DEC_GROUP,) + k_new_cols.shape[1:], lambda i: (i, 0, 0, 0))
        in_specs += [nspec, nspec]
        args += [k_new_cols, v_new_cols]
        out_specs += [wspec, wspec]
        out_shape += [jax.ShapeDtypeStruct(k_win.shape, F32)] * 2
    return pl.pallas_call(
        functools.partial(_swa_decode_kernel, shift=shift, n_groups=bb // DEC_GROUP),
        grid=(nb // bb,),
        in_specs=in_specs,
        out_specs=out_specs,
        out_shape=out_shape,
        compiler_params=_params("parallel"),
        name="swa_decode_shift" if shift else "swa_decode",
    )(*args)


def _rot_heads(w):
    half = HEAD_DIM // 2
    return jnp.concatenate([-w[..., half:], w[..., :half]], axis=-1)


def _rope_tables(positions):
    half = HEAD_DIM // 2
    inv = ROPE_THETA ** (-np.arange(half, dtype=np.float64) / half)
    ang = np.asarray(positions, np.float64)[:, None] * inv[None, :]
    reps = ROPE_TABLE_WIDTH // half
    return (np.tile(np.cos(ang), (1, reps)).astype(np.float32), np.tile(np.sin(ang), (1, reps)).astype(np.float32))


TM_ROWS = 1024
LAYER_TC = 512
DEC_BB = 64
DEC_ATTN_BB = 16


def _to_cols(x, n_heads):
    nb = x.shape[0]
    return x.reshape(nb // DEC_GROUP, DEC_GROUP, n_heads, x.shape[1] // n_heads).transpose(0, 2, 3, 1)


def _to_query_rows(x, n_groups, rows_per_group):
    nb = x.shape[0]
    x4 = x.reshape(nb, rows_per_group, n_groups, HEAD_DIM).transpose(0, 2, 1, 3)
    return jnp.pad(x4, ((0, 0), (0, 0), (0, Q_ROWS - rows_per_group), (0, 0)))


def _from_query_rows(o, rows_per_group):
    nb, n_groups = o.shape[:2]
    return o[:, :, :rows_per_group].transpose(0, 2, 1, 3).reshape(nb, rows_per_group * n_groups * HEAD_DIM)


def kernel(x_prompt, x_sample, cache_mem_k, cache_mem_v, state_hgrn, state_swa_k, state_swa_v, mem_prompt,
           mem_norm, w_mem_kv, w_in_a, lb_logits, hgrn_norm, w_in_b, sinks, kv_norm, w_kv, w_out,
           norm_pre_mix, norm_post_mix, norm_pre_mlp, norm_post_mlp, w_up, w_down):
    bp, seq, d = x_prompt.shape
    bd = x_sample.shape[0]

    w_in_a16 = w_in_a.astype(BF16)
    n_b = DEPTH - N_A_LAYERS
    wq = w_in_b[:, :, :SWA_WIDTH].reshape(n_b, d, SWA_KV_HEADS, SWA_GROUP, HEAD_DIM).transpose(0, 1, 3, 2, 4)
    w_in_b16 = jnp.concatenate([wq.reshape(n_b, d, SWA_WIDTH), _rot_heads(wq).reshape(n_b, d, SWA_WIDTH),
                                w_in_b[:, :, SWA_WIDTH:]], axis=-1).astype(BF16)
    wk = w_kv[:, :SWA_KV_WIDTH].reshape(d, SWA_KV_HEADS, HEAD_DIM)
    w_kv16 = jnp.concatenate([w_kv[:, :SWA_KV_WIDTH], _rot_heads(wk).reshape(d, SWA_KV_WIDTH),
                              w_kv[:, SWA_KV_WIDTH:]], axis=-1).astype(BF16)
    wo_b = w_out[N_A_LAYERS:, :SWA_WIDTH].reshape(n_b, SWA_KV_HEADS, SWA_GROUP, HEAD_DIM, d).transpose(0, 2, 1, 3, 4)
    w_out16 = jnp.concatenate([w_out[:N_A_LAYERS], jnp.concatenate(
        [wo_b.reshape(n_b, SWA_WIDTH, d), w_out[N_A_LAYERS:, SWA_WIDTH:]], axis=1)], axis=0).astype(BF16)
    w_up16 = w_up.astype(BF16)
    w_down16 = w_down.astype(BF16)
    sinks_g = sinks.astype(F32)

    cos_p, sin_p = _rope_tables(np.arange(seq))
    cos_d, sin_d = _rope_tables(np.full((bd,), PAST_LEN))

    mem_rows = mem_prompt.reshape(bp * N_MEM, d)
    mkv = mem_kv(mem_rows, mem_norm, w_mem_kv.astype(BF16))
    mem_k_prompt = mkv[:, :, :XATTN_WIDTH].reshape(DEPTH, bp, N_MEM, XATTN_HEADS, HEAD_DIM)
    mem_v_prompt = mkv[:, :, XATTN_WIDTH:].reshape(DEPTH, bp, N_MEM, XATTN_HEADS, HEAD_DIM)
    mk16 = mkv[:, :, :XATTN_WIDTH].reshape(DEPTH, bp, N_MEM, XATTN_WIDTH).astype(BF16)
    mv16 = mkv[:, :, XATTN_WIDTH:].reshape(DEPTH, bp, N_MEM, XATTN_WIDTH).astype(BF16)

    h = x_prompt.reshape(bp * seq, d)
    hgrn_prompt_states = []
    k_new = v_new = None
    for l in range(DEPTH):
        post = (l, w_out16, norm_post_mix[l], norm_pre_mlp[l], w_up16, w_down16, norm_post_mlp[l])
        if l < N_A_LAYERS:
            proj = norm_matmul(h, norm_pre_mix[l], w_in_a16, l, TM_ROWS)
            h, s_fin = hgrn_layer_prompt(proj, h, lb_logits, hgrn_norm[l], l, mk16[l], mv16[l], *post, bp, seq,
                                         LAYER_TC)
            hgrn_prompt_states.append(s_fin)
        else:
            j = l - N_A_LAYERS
            if j == 0:
                proj, k_new, v_new = norm_matmul_kv(h, norm_pre_mix[l], w_in_b16, j, kv_norm, w_kv16, cos_p, sin_p,
                                                    TM_ROWS)
                k_new = k_new.reshape(bp, seq, SWA_KV_WIDTH)
                v_new = v_new.reshape(bp, seq, SWA_KV_WIDTH)
            else:
                proj = norm_matmul(h, norm_pre_mix[l], w_in_b16, j, TM_ROWS)
            h = swa_layer_prompt(proj, h, k_new, v_new, cos_p, sin_p, sinks_g[j], mk16[l], mv16[l], *post, bp, seq,
                                 LAYER_TC)
    y_prompt = h.reshape(bp, seq, d)
    hgrn_prompt_out = jnp.stack(hgrn_prompt_states)
    swa_k_prompt = k_new[:, -WINDOW:].reshape(bp, WINDOW, SWA_KV_HEADS, HEAD_DIM)
    swa_v_prompt = v_new[:, -WINDOW:].reshape(bp, WINDOW, SWA_KV_HEADS, HEAD_DIM)

    cache_k = cache_mem_k.transpose(0, 1, 3, 4, 2)
    cache_v = cache_mem_v.transpose(0, 1, 3, 4, 2)
    k_win = state_swa_k.transpose(0, 2, 3, 1)
    v_win = state_swa_v.transpose(0, 2, 3, 1)
    cos_row, sin_row = cos_d[:1, :HEAD_DIM], sin_d[:1, :HEAD_DIM]

    h = x_sample.reshape(bd, d)
    hgrn_sample = None
    for l in range(DEPTH):
        if l < N_A_LAYERS:
            proj = norm_matmul(h, norm_pre_mix[l], w_in_a16, l, bd)
            tok, hgrn_sample = hgrn_decode(
                proj, _to_cols(proj[:, :HGRN_WIDTH], HGRN_HEADS),
                _to_cols(proj[:, HGRN_WIDTH:2 * HGRN_WIDTH], HGRN_HEADS), lb_logits, hgrn_norm[l], state_hgrn, l,
                DEC_BB, new_states=hgrn_sample)
            cq = proj[:, 4 * HGRN_WIDTH:]
        else:
            j = l - N_A_LAYERS
            if j == 0:
                proj, k_tok, v_tok = norm_matmul_kv(h, norm_pre_mix[l], w_in_b16, j, kv_norm, w_kv16, cos_d, sin_d, bd)
            else:
                proj = norm_matmul(h, norm_pre_mix[l], w_in_b16, j, bd)
            qp_rows = _to_query_rows(proj[:, :SWA_WIDTH], SWA_KV_HEADS, SWA_GROUP)
            qr_rows = _to_query_rows(proj[:, SWA_WIDTH:2 * SWA_WIDTH], SWA_KV_HEADS, SWA_GROUP)
            if j == 0:
                o_rows, k_win, v_win = swa_decode(sinks_g[j], qp_rows, qr_rows, cos_row, sin_row, k_win, v_win,
                                                  DEC_ATTN_BB, _to_cols(k_tok, SWA_KV_HEADS),
                                                  _to_cols(v_tok, SWA_KV_HEADS))
            else:
                o_rows = swa_decode(sinks_g[j], qp_rows, qr_rows, cos_row, sin_row, k_win, v_win, DEC_ATTN_BB)[0]
            tok = _from_query_rows(o_rows, SWA_GROUP)
            cq = proj[:, 2 * SWA_WIDTH:]
        xo = _from_query_rows(mem_attn_decode(_to_query_rows(cq, XATTN_HEADS, 1), cache_k, cache_v, l, DEC_ATTN_BB), 1)
        h = post_mix(tok, xo, h, l, w_out16, norm_post_mix[l], norm_pre_mlp[l], w_up16, w_down16,
                     norm_post_mlp[l], bd)
    y_sample = h.reshape(bd, 1, d)
    swa_k_sample = k_win.transpose(0, 3, 1, 2)
    swa_v_sample = v_win.transpose(0, 3, 1, 2)

    return (y_prompt, y_sample, mem_k_prompt, mem_v_prompt, hgrn_prompt_out, swa_k_prompt, swa_v_prompt,
            hgrn_sample, swa_k_sample, swa_v_sample)
```
